```python
import math
import jax, jax.numpy as jnp
from jax import lax
import numpy as np

D_MODEL = 1024
BATCH = 8
SEQ = 8192
DEPTH = 1
DEC_BATCH = 32
DEC_SEQ = 16
PAST_LEN = 1024

CHUNK = 64
N_HEADS = 16
HEAD_DIM = 64
KV_HEADS = 4
GROUP = N_HEADS // KV_HEADS
IDX_HEADS = 8
IDX_DIM = 64
TOPK_MAX = 256
Q_BLOCK = 128
GM_CHUNK = 128
GM_GROUPS = 4
GM_WIDTH = D_MODEL
GM_GROUP_DIM = GM_WIDTH // GM_GROUPS
D_FF = 2816
REL_BUCKETS = 32
REL_MAX_DIST = 128
LN_EPS = 1e-5
ALPHA = (2 * DEPTH) ** 0.25
BETA = (8 * DEPTH) ** -0.25
ATT_Q = N_HEADS * HEAD_DIM
ATT_KV = KV_HEADS * HEAD_DIM
IDX_Q = IDX_HEADS * IDX_DIM
IN_SPLITS = (ATT_Q, ATT_KV, ATT_KV, IDX_Q, IDX_DIM, IDX_HEADS, 2 * GM_WIDTH, D_MODEL, D_MODEL)
IN_WIDTH = ATT_Q + 2 * ATT_KV + IDX_Q + IDX_DIM + IDX_HEADS + 2 * GM_WIDTH + 2 * D_MODEL

kernel_name = "dsa_gmlp_gated_streaming_encoder_step"


def layer_norm(x, g, b):
    xf = x.astype(jnp.float32)
    mu = jnp.mean(xf, axis=-1, keepdims=True)
    var = jnp.mean(jnp.square(xf - mu), axis=-1, keepdims=True)
    return ((xf - mu) * lax.rsqrt(var + LN_EPS)).astype(x.dtype) * g + b


def swiglu(x, wg, wu, wd):
    return (jax.nn.silu(x @ wg) * (x @ wu)) @ wd


def rel_bucket(rel):
    half = REL_BUCKETS // 2
    max_exact = half // 2
    ret = jnp.where(rel > 0, half, 0)
    n = jnp.abs(rel)
    nf = jnp.maximum(n, 1).astype(jnp.float32)
    large = max_exact + (jnp.log(nf / max_exact) / math.log(REL_MAX_DIST / max_exact)
                         * (half - max_exact)).astype(jnp.int32)
    large = jnp.minimum(large, half - 1)
    return ret + jnp.where(n < max_exact, n, large)


def dsa_attend(q, qi, wi, q_pos, n_vis, k, v, ki, rel_table, n_sel):
    f32 = jnp.float32
    B, Q = q.shape[:2]
    L = k.shape[1]
    dots = jnp.einsum('bqhd,bsd->bqhs', qi, ki).astype(f32) * IDX_DIM ** -0.5
    score = jnp.einsum('bqhs,bqh->bqs', jax.nn.relu(dots), wi.astype(f32)) * IDX_HEADS ** -0.5
    key_ok = jnp.arange(L, dtype=jnp.int32)[None, :] < n_vis[:, None]
    score = jnp.where(key_ok[None], score, -jnp.inf)
    _, sel = lax.top_k(score, n_sel)
    valid = sel < n_vis[None, :, None]
    gather = jax.vmap(lambda rows, ids: rows[ids])
    kg = gather(k, sel)
    vg = gather(v, sel)
    qg = q.reshape(B, Q, KV_HEADS, GROUP, HEAD_DIM)
    logits = jnp.einsum('bqkgd,bqnkd->bqkgn', qg, kg).astype(f32) * HEAD_DIM ** -0.5
    bias = rel_table[rel_bucket(sel - q_pos[None, :, None])].astype(f32)
    bias = bias.reshape(B, Q, n_sel, KV_HEADS, GROUP).transpose(0, 1, 3, 4, 2)
    logits = jnp.where(valid[:, :, None, None, :], logits + bias, -1e30)
    p = jax.nn.softmax(logits, axis=-1).astype(v.dtype)
    out = jnp.einsum('bqkgn,bqnkd->bqkgd', p, vg)
    return out.reshape(B, Q, N_HEADS * HEAD_DIM)


def dsa_prompt(q, qi, wi, k, v, ki, rel_table):
    B, S = q.shape[:2]
    nb = S // Q_BLOCK
    n_sel = min(TOPK_MAX, S // 4)
    pos = jnp.arange(S, dtype=jnp.int32)
    n_vis = (pos // CHUNK + 1) * CHUNK

    def blk(a):
        return jnp.swapaxes(a.reshape(B, nb, Q_BLOCK, *a.shape[2:]), 0, 1)

    def one(args):
        qb, qib, wib, pb, vb = args
        return dsa_attend(qb, qib, wib, pb, vb, k, v, ki, rel_table, n_sel)

    out = lax.map(one, (blk(q), blk(qi), blk(wi), pos.reshape(nb, Q_BLOCK), n_vis.reshape(nb, Q_BLOCK)))
    return jnp.swapaxes(out, 0, 1).reshape(B, S, N_HEADS * HEAD_DIM)


def spatial_gate(u, v, w_s, b_s):
    B, T = u.shape[:2]
    rows = min(T, GM_CHUNK)
    n = T // rows
    mask = jnp.tril(jnp.ones((GM_CHUNK, GM_CHUNK), w_s.dtype))
    ws = (w_s * mask)[:, :rows, :rows]
    vr = v.reshape(B, n, rows, GM_GROUPS, GM_GROUP_DIM)
    s = jnp.einsum('gij,bnjgc->bnigc', ws, vr) + b_s[:, :rows].T[None, None, :, :, None]
    return u * s.reshape(B, T, GM_WIDTH)


def mixer_inputs(h, w_in, gm_ln_g, gm_ln_b):
    z = h @ w_in
    parts = []
    off = 0
    for w in IN_SPLITS:
        parts.append(z[..., off:off + w])
        off += w
    q, k, v, qi, ki, wi, zg, ga, gb = parts
    B, T = h.shape[:2]
    q = q.reshape(B, T, N_HEADS, HEAD_DIM)
    k = k.reshape(B, T, KV_HEADS, HEAD_DIM)
    v = v.reshape(B, T, KV_HEADS, HEAD_DIM)
    qi = qi.reshape(B, T, IDX_HEADS, IDX_DIM)
    zg = jax.nn.gelu(zg)
    u = zg[..., :GM_WIDTH]
    vg = layer_norm(zg[..., GM_WIDTH:], gm_ln_g, gm_ln_b)
    return q, k, v, qi, ki, wi, u, vg, ga, gb


def run_layer(x, mixer, ln1_g, ln1_b, f1g, f1u, f1d, w_in, gm_ln_g, gm_ln_b, gm_ws, gm_bs,
              w_br_a, w_br_b, w_out, ln2_g, ln2_b, f2g, f2u, f2d, ln3_g, ln3_b):
    h = layer_norm(ALPHA * x + 0.5 * swiglu(x, f1g, f1u, f1d), ln1_g, ln1_b)
    q, k, v, qi, ki, wi, u, vg, ga, gb = mixer_inputs(h, w_in, gm_ln_g, gm_ln_b)
    a, rows = mixer(q, k, v, qi, ki, wi)
    g = spatial_gate(u, vg, gm_ws, gm_bs)
    mixed = (jax.nn.sigmoid(ga) * (a @ w_br_a) + jax.nn.sigmoid(gb) * (g @ w_br_b)) @ w_out
    h = layer_norm(ALPHA * h + mixed, ln2_g, ln2_b)
    y = layer_norm(ALPHA * h + 0.5 * swiglu(h, f2g, f2u, f2d), ln3_g, ln3_b)
    return y, rows, vg


def setup_inputs(seed: int = 0) -> dict:
    key = jax.random.key(seed)
    ks = iter(jax.random.split(key, 32))

    def nrm(shape, scale):
        return jax.random.normal(next(ks), shape, jnp.float32) * scale

    col_scale = jnp.ones((IN_WIDTH,), jnp.float32).at[ATT_Q + ATT_KV:ATT_Q + 2 * ATT_KV].set(BETA)
    return {
        "x_prompt": nrm((BATCH, SEQ, D_MODEL), 1.0),
        "x_sample": nrm((DEC_BATCH, DEC_SEQ, D_MODEL), 1.0),
        "cache_k": nrm((DEPTH, DEC_BATCH, PAST_LEN, KV_HEADS, HEAD_DIM), 1.0),
        "cache_v": nrm((DEPTH, DEC_BATCH, PAST_LEN, KV_HEADS, HEAD_DIM), BETA),
        "cache_kidx": nrm((DEPTH, DEC_BATCH, PAST_LEN, IDX_DIM), 1.0),
        "rel_table": nrm((REL_BUCKETS, N_HEADS), 0.5),
        "ln1_g": 1.0 + nrm((DEPTH, D_MODEL), 0.02),
        "ln1_b": nrm((DEPTH, D_MODEL), 0.02),
        "ffn1_wg": nrm((DEPTH, D_MODEL, D_FF), D_MODEL ** -0.5),
        "ffn1_wu": nrm((DEPTH, D_MODEL, D_FF), D_MODEL ** -0.5),
        "ffn1_wd": nrm((DEPTH, D_FF, D_MODEL), D_FF ** -0.5 * BETA),
        "w_in": nrm((DEPTH, D_MODEL, IN_WIDTH), D_MODEL ** -0.5) * col_scale,
        "gm_ln_g": 1.0 + nrm((DEPTH, GM_WIDTH), 0.02),
        "gm_ln_b": nrm((DEPTH, GM_WIDTH), 0.02),
        "gm_ws": nrm((DEPTH, GM_GROUPS, GM_CHUNK, GM_CHUNK), GM_CHUNK ** -0.5),
        "gm_bs": 1.0 + nrm((DEPTH, GM_GROUPS, GM_CHUNK), 0.02),
        "w_br_a": nrm((DEPTH, ATT_Q, D_MODEL), ATT_Q ** -0.5 * BETA),
        "w_br_b": nrm((DEPTH, GM_WIDTH, D_MODEL), GM_WIDTH ** -0.5 * BETA),
        "w_out": nrm((DEPTH, D_MODEL, D_MODEL), D_MODEL ** -0.5 * BETA),
        "ln2_g": 1.0 + nrm((DEPTH, D_MODEL), 0.02),
        "ln2_b": nrm((DEPTH, D_MODEL), 0.02),
        "ffn2_wg": nrm((DEPTH, D_MODEL, D_FF), D_MODEL ** -0.5),
        "ffn2_wu": nrm((DEPTH, D_MODEL, D_FF), D_MODEL ** -0.5),
        "ffn2_wd": nrm((DEPTH, D_FF, D_MODEL), D_FF ** -0.5 * BETA),
        "ln3_g": 1.0 + nrm((DEPTH, D_MODEL), 0.02),
        "ln3_b": nrm((DEPTH, D_MODEL), 0.02),
    }


def reference(x_prompt, x_sample, cache_k, cache_v, cache_kidx, rel_table,
              ln1_g, ln1_b, ffn1_wg, ffn1_wu, ffn1_wd, w_in, gm_ln_g, gm_ln_b, gm_ws, gm_bs,
              w_br_a, w_br_b, w_out, ln2_g, ln2_b, ffn2_wg, ffn2_wu, ffn2_wd, ln3_g, ln3_b):
    past = cache_k.shape[2]
    n_new = x_sample.shape[1]
    total = past + n_new
    n_sel_sample = min(TOPK_MAX, total // 4)
    pos_s = past + jnp.arange(n_new, dtype=jnp.int32)
    vis_s = jnp.full((n_new,), total, jnp.int32)

    xp, xs = x_prompt, x_sample
    kp_l, vp_l, kip_l, ks_l, vs_l, kis_l, gv_l = [], [], [], [], [], [], []
    for l in range(DEPTH):
        lw = (ln1_g[l], ln1_b[l], ffn1_wg[l], ffn1_wu[l], ffn1_wd[l], w_in[l], gm_ln_g[l], gm_ln_b[l],
              gm_ws[l], gm_bs[l], w_br_a[l], w_br_b[l], w_out[l], ln2_g[l], ln2_b[l],
              ffn2_wg[l], ffn2_wu[l], ffn2_wd[l], ln3_g[l], ln3_b[l])

        def prompt_mixer(q, k, v, qi, ki, wi):
            return dsa_prompt(q, qi, wi, k, v, ki, rel_table), (k, v, ki)

        def sample_mixer(q, k, v, qi, ki, wi, l=l):
            k_all = jnp.concatenate([cache_k[l].astype(k.dtype), k], axis=1)
            v_all = jnp.concatenate([cache_v[l].astype(v.dtype), v], axis=1)
            ki_all = jnp.concatenate([cache_kidx[l].astype(ki.dtype), ki], axis=1)
            a = dsa_attend(q, qi, wi, pos_s, vis_s, k_all, v_all, ki_all, rel_table, n_sel_sample)
            return a, (k, v, ki)

        xp, (kp, vp, kip), _ = run_layer(xp, prompt_mixer, *lw)
        xs, (ks_, vs_, kis), gv = run_layer(xs, sample_mixer, *lw)
        kp_l.append(kp); vp_l.append(vp); kip_l.append(kip)
        ks_l.append(ks_); vs_l.append(vs_); kis_l.append(kis); gv_l.append(gv)

    return (xp, xs, jnp.stack(kp_l), jnp.stack(vp_l), jnp.stack(kip_l),
            jnp.stack(ks_l), jnp.stack(vs_l), jnp.stack(kis_l), jnp.stack(gv_l))
```

```python
import functools
import math

import numpy as np
import jax
import jax.numpy as jnp
from jax import lax
from jax.experimental import pallas as pl
from jax.experimental.pallas import tpu as pltpu

D_MODEL = 1024
CHUNK = 64
N_HEADS = 16
HEAD_DIM = 64
KV_HEADS = 4
GROUP = N_HEADS // KV_HEADS
IDX_HEADS = 8
IDX_DIM = 64
TOPK_MAX = 256
GM_CHUNK = 128
GM_GROUPS = 4
GM_GROUP_DIM = D_MODEL // GM_GROUPS
D_FF = 2816
REL_BUCKETS = 32
REL_MAX_DIST = 128
LN_EPS = 1e-5
ATT_Q = N_HEADS * HEAD_DIM
ATT_KV = KV_HEADS * HEAD_DIM
IDX_Q = IDX_HEADS * IDX_DIM

V7X_LANES = 128
V7X_MXU_DIM = 256
V7X_VMEM_BYTES = 64 * 1024 * 1024

FF_BLOCK = V7X_MXU_DIM
NEG_LOGIT = -1e30
INT_MIN = -2 ** 31
F32 = jnp.float32
BF16 = jnp.bfloat16


def _vmem_limit(nbytes):
    return int(min(max(nbytes, 16 * 1024 * 1024), V7X_VMEM_BYTES - 6 * 1024 * 1024))


def _resident(shape):
    zeros = (0,) * len(shape)
    return pl.BlockSpec(shape, lambda *_: zeros, pipeline_mode=pl.Buffered(1))


def _layer_norm(x, g, b):
    mu = jnp.mean(x, axis=-1, keepdims=True)
    xc = x - mu
    var = jnp.mean(xc * xc, axis=-1, keepdims=True)
    return xc * lax.rsqrt(var + LN_EPS) * g + b


def _swiglu(xb, wg_ref, wu_ref, wd_ref, act_scr):
    for c in range(D_FF // FF_BLOCK):
        cols = slice(c * FF_BLOCK, (c + 1) * FF_BLOCK)
        g = jnp.dot(xb, wg_ref[:, cols], preferred_element_type=F32)
        u = jnp.dot(xb, wu_ref[:, cols], preferred_element_type=F32)
        act_scr[:, cols] = (g * jax.nn.sigmoid(g) * u).astype(BF16)
    return jnp.dot(act_scr[...], wd_ref[...], preferred_element_type=F32)


def _ffn_ln_kernel(x_ref, wg_ref, wu_ref, wd_ref, g_ref, b_ref, o_ref, act_scr, *, alpha):
    x = x_ref[...]
    y = _swiglu(x.astype(BF16), wg_ref, wu_ref, wd_ref, act_scr)
    o_ref[...] = _layer_norm(alpha * x + 0.5 * y, g_ref[...], b_ref[...])


def _ffn_ln(x, wg, wu, wd, g, b, *, tm, alpha):
    m = x.shape[0]
    row = lambda i: (i, 0)
    weights = 3 * D_MODEL * D_FF * 2
    tiles = tm * D_MODEL * 4 * 4 + tm * D_FF * (2 + 3 * 4)
    return pl.pallas_call(
        functools.partial(_ffn_ln_kernel, alpha=alpha),
        out_shape=jax.ShapeDtypeStruct((m, D_MODEL), F32),
        grid=(m // tm,),
        in_specs=[pl.BlockSpec((tm, D_MODEL), row),
                  _resident((D_MODEL, D_FF)), _resident((D_MODEL, D_FF)), _resident((D_FF, D_MODEL)),
                  _resident((1, D_MODEL)), _resident((1, D_MODEL))],
        out_specs=pl.BlockSpec((tm, D_MODEL), row),
        scratch_shapes=[pltpu.VMEM((tm, D_FF), BF16)],
        compiler_params=pltpu.CompilerParams(dimension_semantics=("arbitrary",),
                                             vmem_limit_bytes=_vmem_limit(weights + tiles)),
        name="ffn_ln",
    )(x, wg, wu, wd, g, b)


def _mix_in_kernel(h_ref, wqkv_ref, widx_ref, wzg_ref, wga_ref, wgb_ref, gmg_ref, gmb_ref, ws_ref, bst_ref,
                   q_ref, kf_ref, vf_ref, kif_ref, kb_ref, vb_ref, kib_ref, qi_ref, wi_ref, gate_ref,
                   ga_ref, gb_ref, *maybe_vg_ref, tm, gate_rows):
    hb = h_ref[...].astype(BF16)

    qkv = jnp.dot(hb, wqkv_ref[...], preferred_element_type=F32)
    q_ref[...] = (qkv[:, :ATT_Q] * HEAD_DIM ** -0.5).astype(BF16)
    k = qkv[:, ATT_Q:ATT_Q + ATT_KV]
    v = qkv[:, ATT_Q + ATT_KV:]
    kf_ref[...] = k
    vf_ref[...] = v
    kb_ref[...] = k.astype(BF16)
    vb_ref[...] = v.astype(BF16)

    idx = jnp.dot(hb, widx_ref[...], preferred_element_type=F32)
    qi_ref[...] = idx[:, :IDX_Q].astype(BF16)
    ki = idx[:, IDX_Q:IDX_Q + IDX_DIM]
    kif_ref[...] = ki
    kib_ref[...] = ki.astype(BF16)
    wi_ref[...] = idx[:, IDX_Q + V7X_LANES:]

    ga_ref[...] = jnp.dot(hb, wga_ref[...], preferred_element_type=F32)
    gb_ref[...] = jnp.dot(hb, wgb_ref[...], preferred_element_type=F32)

    zg = jax.nn.gelu(jnp.dot(hb, wzg_ref[...], preferred_element_type=F32))
    u = zg[:, :D_MODEL]
    vg = _layer_norm(zg[:, D_MODEL:], gmg_ref[...], gmb_ref[...])
    if maybe_vg_ref:
        maybe_vg_ref[0][...] = vg
    vgb = vg.astype(BF16)

    r = lax.broadcasted_iota(jnp.int32, (GM_CHUNK, GM_CHUNK), 0)
    c = lax.broadcasted_iota(jnp.int32, (GM_CHUNK, GM_CHUNK), 1)
    keep = (r >= c) & ((r // gate_rows) == (c // gate_rows))
    for grp in range(GM_GROUPS):
        cols = slice(grp * GM_GROUP_DIM, (grp + 1) * GM_GROUP_DIM)
        w = jnp.where(keep, ws_ref[grp], 0.0).astype(BF16)
        bias = bst_ref[:, grp:grp + 1]
        for ch in range(tm // GM_CHUNK):
            rows = slice(ch * GM_CHUNK, (ch + 1) * GM_CHUNK)
            s = jnp.dot(w, vgb[rows, cols], preferred_element_type=F32) + bias
            gate_ref[rows, cols] = (u[rows, cols] * s).astype(BF16)


def _mix_in(h, wqkv, widx, wzg, wga, wgb, gmg, gmb, ws, bst, *, tm, gate_rows, emit_vg):
    m = h.shape[0]
    row = lambda i: (i, 0)
    out = lambda n, dt: (jax.ShapeDtypeStruct((m, n), dt), pl.BlockSpec((tm, n), row))
    outs = [out(ATT_Q, BF16), out(ATT_KV, F32), out(ATT_KV, F32), out(IDX_DIM, F32),
            out(ATT_KV, BF16), out(ATT_KV, BF16), out(IDX_DIM, BF16), out(IDX_Q, BF16),
            out(V7X_LANES, F32), out(D_MODEL, BF16), out(D_MODEL, F32), out(D_MODEL, F32)]
    if emit_vg:
        outs.append(out(D_MODEL, F32))
    n_in = wqkv.shape[1] + widx.shape[1] + wzg.shape[1] + 2 * D_MODEL
    weights = D_MODEL * n_in * 2
    tiles = tm * (n_in * 4 * 3 + D_MODEL * 4 * 2)
    return pl.pallas_call(
        functools.partial(_mix_in_kernel, tm=tm, gate_rows=gate_rows),
        out_shape=[o[0] for o in outs],
        grid=(m // tm,),
        in_specs=[pl.BlockSpec((tm, D_MODEL), row),
                  _resident(wqkv.shape), _resident(widx.shape), _resident(wzg.shape),
                  _resident(wga.shape), _resident(wgb.shape),
                  _resident((1, D_MODEL)), _resident((1, D_MODEL)),
                  _resident(ws.shape), _resident(bst.shape)],
        out_specs=[o[1] for o in outs],
        compiler_params=pltpu.CompilerParams(dimension_semantics=("arbitrary",),
                                             vmem_limit_bytes=_vmem_limit(weights + tiles)),
        name="mix_in",
    )(h, wqkv, widx, wzg, wga, wgb, gmg, gmb, ws, bst)


def _bucket_thresholds():
    half = REL_BUCKETS // 2
    max_exact = half // 2
    n = np.arange(1, 2 * REL_MAX_DIST, dtype=np.float32)
    large = max_exact + (np.log(n / np.float32(max_exact)) / np.float32(math.log(REL_MAX_DIST / max_exact))
                         * np.float32(half - max_exact)).astype(np.int32)
    large = np.minimum(large, half - 1)
    thr = [int(np.argmax(large >= b)) + 1 for b in range(max_exact + 1, half)]
    return max_exact, half, thr


def _bias_kernel(table_ref, o_ref, *, tq, tk):
    max_exact, half, thr = _bucket_thresholds()
    d = pl.program_id(0)
    h = pl.program_id(1)
    rel = (lax.broadcasted_iota(jnp.int32, (tq, tk), 1) - lax.broadcasted_iota(jnp.int32, (tq, tk), 0)
           + (d - 1) * tk)
    n = jnp.abs(rel)
    large = jnp.full((tq, tk), max_exact, jnp.int32)
    for t in thr:
        large = large + jnp.where(n >= t, 1, 0)
    bucket = jnp.where(rel > 0, half, 0) + jnp.where(n < max_exact, n, large)
    acc = jnp.zeros((tq, tk), F32)
    for b in range(REL_BUCKETS):
        acc = acc + jnp.where(bucket == b, table_ref[b, h], 0.0)
    o_ref[...] = acc - table_ref[half - 1, h]


def _bias_tiles(rel_table, tq, tk):
    return pl.pallas_call(
        functools.partial(_bias_kernel, tq=tq, tk=tk),
        out_shape=jax.ShapeDtypeStruct((2, N_HEADS, tq, tk), F32),
        grid=(2, N_HEADS),
        in_specs=[pl.BlockSpec(memory_space=pltpu.SMEM)],
        out_specs=pl.BlockSpec((None, None, tq, tk), lambda d, h: (d, h, 0, 0)),
        compiler_params=pltpu.CompilerParams(dimension_semantics=("arbitrary", "arbitrary")),
        name="rel_bias",
    )(rel_table)


def _dsa_kernel(q_ref, qi_ref, wi_ref, k_ref, v_ref, ki_ref, bias_ref, o_ref,
                keys_scr, qstk_scr, qistk_scr, wib_scr, m_scr, l_scr, acc_scr,
                *, tq, tk, n_sel, causal, q_start, n_keys):
    f_sel = float(n_sel)
    if causal:
        j = pl.program_id(1)
        nkb = j + 1
        pos = j * tq + lax.broadcasted_iota(jnp.int32, (tq, 1), 0)
        n_vis = (pos // CHUNK + 1) * CHUNK
    else:
        nkb = -(-n_keys // tk)
        n_vis = jnp.full((tq, 1), n_keys, jnp.int32)

    for hh in range(N_HEADS):
        g, i = divmod(hh, GROUP)
        qstk_scr[g, i * tq:(i + 1) * tq, :] = q_ref[:, hh * HEAD_DIM:(hh + 1) * HEAD_DIM]
    for h in range(IDX_HEADS):
        qistk_scr[h * tq:(h + 1) * tq, :] = qi_ref[:, h * IDX_DIM:(h + 1) * IDX_DIM]
        wib_scr[h] = jnp.broadcast_to(wi_ref[:, h:h + 1], (tq, tk))

    nt_dims = (((1,), (1,)), ((), ()))

    def score_body(kb, carry):
        off = pl.multiple_of(kb * tk, tk)
        d = lax.dot_general(qistk_scr[...], ki_ref[pl.ds(off, tk), :], nt_dims, preferred_element_type=F32)
        s = jnp.zeros((tq, tk), F32)
        for h in range(IDX_HEADS):
            s = s + wib_scr[h] * jnp.maximum(d[h * tq:(h + 1) * tq], 0.0)
        s = jnp.where(s == 0.0, 0.0, s)
        bits = pltpu.bitcast(s, jnp.int32)
        key = bits ^ ((bits >> 31) & 0x7FFFFFFF)
        kpos = off + lax.broadcasted_iota(jnp.int32, (tq, tk), 1)
        keys_scr[kb] = jnp.where(kpos < n_vis, key, INT_MIN)
        return carry

    lax.fori_loop(0, nkb, score_body, 0)

    def count_ge(cand):
        candb = jnp.broadcast_to(cand, (tq, V7X_LANES))

        def body(kb, part):
            t = keys_scr[kb]
            for c in range(tk // V7X_LANES):
                part = part + jnp.where(t[:, c * V7X_LANES:(c + 1) * V7X_LANES] >= candb, 1.0, 0.0)
            return part

        part = lax.fori_loop(0, nkb, body, jnp.zeros((tq, V7X_LANES), F32))
        return jnp.sum(part, axis=1, keepdims=True)

    def bit_body(i, carry):
        prefix, cnt_t = carry
        cand = prefix + lax.shift_left(jnp.int32(1), (31 - i).astype(jnp.int32))
        cnt = count_ge(cand)
        take = cnt >= f_sel
        return jnp.where(take, cand, prefix), jnp.where(take, cnt, cnt_t)

    thr, cnt_thr = lax.fori_loop(0, 32, bit_body,
                                 (jnp.full((tq, 1), INT_MIN, jnp.int32), jnp.zeros((tq, 1), F32)))

    tie = (n_vis > n_sel) & (cnt_thr != f_sel)
    any_tie = jnp.max(jnp.where(tie, 1.0, 0.0)) > 0.0

    @pl.when(any_tie)
    def _():
        room = f_sel - count_ge(thr + 1)
        rr = lax.broadcasted_iota(jnp.int32, (tk, tk), 0)
        cc = lax.broadcasted_iota(jnp.int32, (tk, tk), 1)
        upper = jnp.where(rr <= cc, 1.0, 0.0).astype(BF16)

        def body(kb, seen):
            t = keys_scr[kb]
            eq = t == thr
            eqf = jnp.where(eq, 1.0, 0.0)
            rank = jnp.dot(eqf.astype(BF16), upper, preferred_element_type=F32) + seen
            keys_scr[kb] = jnp.where(eq & tie & (rank > room), thr - 1, t)
            return seen + jnp.sum(eqf, axis=1, keepdims=True)

        lax.fori_loop(0, nkb, body, jnp.zeros((tq, 1), F32))

    thr_sel = jnp.maximum(thr, INT_MIN + 1)

    m_scr[...] = jnp.full(m_scr.shape, NEG_LOGIT, F32)
    l_scr[...] = jnp.zeros(l_scr.shape, F32)
    acc_scr[...] = jnp.zeros(acc_scr.shape, F32)

    def attend(kb, near):
        off = pl.multiple_of(kb * tk, tk)
        mask = (keys_scr[kb] >= thr_sel)[None]
        for g in range(KV_HEADS):
            cols = slice(g * HEAD_DIM, (g + 1) * HEAD_DIM)
            s = lax.dot_general(qstk_scr[g], k_ref[pl.ds(off, tk), cols], nt_dims, preferred_element_type=F32)
            s = s.reshape(GROUP, tq, tk)
            if near is not None:
                s = s + bias_ref[near, g * GROUP:(g + 1) * GROUP]
            s = jnp.where(mask, s, NEG_LOGIT)
            m_old = m_scr[g]
            m_new = jnp.maximum(m_old, jnp.max(s, axis=-1, keepdims=True))
            alpha = jnp.exp(m_old - m_new)
            p = jnp.exp(s - m_new)
            l_scr[g] = alpha * l_scr[g] + jnp.sum(p, axis=-1, keepdims=True)
            pv = jnp.dot(p.astype(BF16).reshape(GROUP * tq, tk), v_ref[pl.ds(off, tk), cols],
                         preferred_element_type=F32)
            acc_scr[g] = alpha.reshape(GROUP * tq, 1) * acc_scr[g] + pv
            m_scr[g] = m_new

    n_far = jnp.maximum(nkb - 2, 0)

    def far_body(kb, carry):
        attend(kb, None)
        return carry

    def near_body(kb, carry):
        attend(kb, kb - (nkb - 2))
        return carry

    lax.fori_loop(0, n_far, far_body, 0)
    lax.fori_loop(n_far, nkb, near_body, 0)

    for hh in range(N_HEADS):
        g, i = divmod(hh, GROUP)
        o = acc_scr[g, i * tq:(i + 1) * tq, :] / l_scr[g, i]
        o_ref[:, hh * HEAD_DIM:(hh + 1) * HEAD_DIM] = o.astype(BF16)


def _dsa(q, qi, wi, k, v, ki, bias, *, tq, tk, n_sel, causal, q_start, n_keys):
    nb, t = q.shape[:2]
    lp = k.shape[1]
    qmap = lambda b, j: (b, j, 0)
    kmap = lambda b, j: (b, 0, 0)
    kv_spec = lambda w: pl.BlockSpec((None, lp, w), kmap, pipeline_mode=pl.Buffered(1))
    nkb = lp // tk
    assert tk >= REL_MAX_DIST and lp % tk == 0
    assert tq == tk if causal else q_start == (nkb - 1) * tk
    scratch = [pltpu.VMEM((lp // tk, tq, tk), jnp.int32),
               pltpu.VMEM((KV_HEADS, GROUP * tq, HEAD_DIM), BF16),
               pltpu.VMEM((IDX_HEADS * tq, IDX_DIM), BF16),
               pltpu.VMEM((IDX_HEADS, tq, tk), F32),
               pltpu.VMEM((KV_HEADS, GROUP, tq, 1), F32),
               pltpu.VMEM((KV_HEADS, GROUP, tq, 1), F32),
               pltpu.VMEM((KV_HEADS, GROUP * tq, HEAD_DIM), F32)]
    est = (2 * lp * (2 * ATT_KV + V7X_LANES) * 2 + lp * tq * 4 + 2 * N_HEADS * tq * tk * 4
           + 3 * N_HEADS * tq * V7X_LANES * 4 + IDX_HEADS * tq * tk * 4 * 3 + 16 * tq * tk * 4 * 4)
    return pl.pallas_call(
        functools.partial(_dsa_kernel, tq=tq, tk=tk, n_sel=n_sel, causal=causal, q_start=q_start, n_keys=n_keys),
        out_shape=jax.ShapeDtypeStruct((nb, t, ATT_Q), BF16),
        grid=(nb, t // tq),
        in_specs=[pl.BlockSpec((None, tq, ATT_Q), qmap), pl.BlockSpec((None, tq, IDX_Q), qmap),
                  pl.BlockSpec((None, tq, V7X_LANES), qmap),
                  kv_spec(ATT_KV), kv_spec(ATT_KV), kv_spec(IDX_DIM),
                  _resident(bias.shape)],
        out_specs=pl.BlockSpec((None, tq, ATT_Q), qmap),
        scratch_shapes=scratch,
        compiler_params=pltpu.CompilerParams(dimension_semantics=("arbitrary", "arbitrary"),
                                             vmem_limit_bytes=_vmem_limit(est)),
        name="dsa",
    )(q, qi, wi, k, v, ki, bias)


def _merge_ffn_kernel(a_ref, gate_ref, ga_ref, gb_ref, h_ref, wa_ref, wb_ref, wo_ref, g2_ref, b2_ref,
                      wg_ref, wu_ref, wd_ref, g3_ref, b3_ref, o_ref, act_scr, *, alpha):
    br_a = jnp.dot(a_ref[...], wa_ref[...], preferred_element_type=F32)
    br_b = jnp.dot(gate_ref[...], wb_ref[...], preferred_element_type=F32)
    merge = jax.nn.sigmoid(ga_ref[...]) * br_a + jax.nn.sigmoid(gb_ref[...]) * br_b
    mixed = jnp.dot(merge.astype(BF16), wo_ref[...], preferred_element_type=F32)
    h2 = _layer_norm(alpha * h_ref[...] + mixed, g2_ref[...], b2_ref[...])
    y = _swiglu(h2.astype(BF16), wg_ref, wu_ref, wd_ref, act_scr)
    o_ref[...] = _layer_norm(alpha * h2 + 0.5 * y, g3_ref[...], b3_ref[...])


def _merge_ffn(a, gate, ga, gb, h, wa, wb, wo, g2, b2, wg, wu, wd, g3, b3, *, tm, alpha):
    m = h.shape[0]
    row = lambda i: (i, 0)
    tile = lambda: pl.BlockSpec((tm, D_MODEL), row)
    vec = lambda: _resident((1, D_MODEL))
    sq = lambda: _resident((D_MODEL, D_MODEL))
    weights = (3 * D_MODEL * D_MODEL + 3 * D_MODEL * D_FF) * 2
    tiles = tm * D_MODEL * 4 * 12 + tm * D_FF * (2 + 3 * 4)
    return pl.pallas_call(
        functools.partial(_merge_ffn_kernel, alpha=alpha),
        out_shape=jax.ShapeDtypeStruct((m, D_MODEL), F32),
        grid=(m // tm,),
        in_specs=[tile(), tile(), tile(), tile(), tile(), sq(), sq(), sq(), vec(), vec(),
                  _resident((D_MODEL, D_FF)), _resident((D_MODEL, D_FF)), _resident((D_FF, D_MODEL)),
                  vec(), vec()],
        out_specs=tile(),
        scratch_shapes=[pltpu.VMEM((tm, D_FF), BF16)],
        compiler_params=pltpu.CompilerParams(dimension_semantics=("arbitrary",),
                                             vmem_limit_bytes=_vmem_limit(weights + tiles)),
        name="merge_ffn",
    )(a, gate, ga, gb, h, wa, wb, wo, g2, b2, wg, wu, wd, g3, b3)


def _split_w_in(w_in):
    o_k = ATT_Q
    o_qi = ATT_Q + 2 * ATT_KV
    o_ki = o_qi + IDX_Q
    o_wi = o_ki + IDX_DIM
    o_zg = o_wi + IDX_HEADS
    o_ga = o_zg + 2 * D_MODEL
    o_gb = o_ga + D_MODEL
    wb = w_in.astype(BF16)
    pad = lambda w, n: jnp.pad(w, ((0, 0), (0, n - w.shape[1])))
    widx = jnp.concatenate([wb[:, o_qi:o_ki], pad(wb[:, o_ki:o_wi], V7X_LANES),
                            pad(wb[:, o_wi:o_zg], V7X_LANES)], axis=1)
    return wb[:, :o_qi], widx, wb[:, o_zg:o_ga], wb[:, o_ga:o_gb], wb[:, o_gb:]


def kernel(x_prompt, x_sample, cache_k, cache_v, cache_kidx, rel_table, ln1_g, ln1_b, ffn1_wg, ffn1_wu, ffn1_wd, w_in, gm_ln_g, gm_ln_b, gm_ws, gm_bs, w_br_a, w_br_b, w_out, ln2_g, ln2_b, ffn2_wg, ffn2_wu, ffn2_wd, ln3_g, ln3_b):
    depth = ln1_g.shape[0]
    assert depth == 1, "single-layer step"
    alpha = (2 * depth) ** 0.25
    nb, seq, _ = x_prompt.shape
    ns, n_new, _ = x_sample.shape
    past = cache_k.shape[2]
    total = past + n_new
    l = 0
    vec = lambda p: p[l].reshape(1, D_MODEL)
    bf = lambda p: p[l].astype(BF16)

    w_qkv, w_idx, w_zg, w_ga, w_gb = _split_w_in(w_in[l])
    f1 = (bf(ffn1_wg), bf(ffn1_wu), bf(ffn1_wd))
    f2 = (bf(ffn2_wg), bf(ffn2_wu), bf(ffn2_wd))
    br = (bf(w_br_a), bf(w_br_b), bf(w_out))

    def dense_in(x, tm, gate_rows, emit_vg):
        h = _ffn_ln(x, *f1, vec(ln1_g), vec(ln1_b), tm=tm, alpha=alpha)
        reps = GM_CHUNK // gate_rows
        ws = jnp.tile(gm_ws[l][:, :gate_rows, :gate_rows], (1, reps, reps))
        bst = jnp.tile(gm_bs[l][:, :gate_rows].T, (reps, 1))
        tm_mix = min(tm, 512)
        parts = _mix_in(h, w_qkv, w_idx, w_zg, w_ga, w_gb, vec(gm_ln_g), vec(gm_ln_b), ws, bst,
                        tm=tm_mix, gate_rows=gate_rows, emit_vg=emit_vg)
        return h, parts

    def dense_out(a, gate, ga, gb, h, tm):
        return _merge_ffn(a, gate, ga, gb, h, *br, vec(ln2_g), vec(ln2_b), *f2, vec(ln3_g), vec(ln3_b),
                          tm=tm, alpha=alpha)

    mp = nb * seq
    hp, (q, kf, vf, kif, kb, vb, kib, qi, wi, gate, ga, gb) = dense_in(
        x_prompt.reshape(mp, D_MODEL), 512, GM_CHUNK, False)
    tq = 256
    bias_p = _bias_tiles(rel_table, tq, tq)
    a = _dsa(q.reshape(nb, seq, ATT_Q), qi.reshape(nb, seq, IDX_Q), wi.reshape(nb, seq, V7X_LANES),
             kb.reshape(nb, seq, ATT_KV), vb.reshape(nb, seq, ATT_KV), kib.reshape(nb, seq, IDX_DIM), bias_p,
             tq=tq, tk=tq, n_sel=min(TOPK_MAX, seq // 4), causal=True, q_start=0, n_keys=seq)
    y_p = dense_out(a.reshape(mp, ATT_Q), gate, ga, gb, hp, 512).reshape(nb, seq, D_MODEL)

    ms = ns * n_new
    hs, (q, kfs, vfs, kifs, kb, vb, kib, qi, wi, gate, ga, gb, vg) = dense_in(
        x_sample.reshape(ms, D_MODEL), 128, n_new, True)
    tk_s = V7X_LANES
    lp = -(-total // tk_s) * tk_s
    cat = lambda c, new, w: jnp.pad(
        jnp.concatenate([c[l].reshape(ns, past, w).astype(BF16), new.reshape(ns, n_new, w)], axis=1),
        ((0, 0), (0, lp - total), (0, 0)))
    bias_s = _bias_tiles(rel_table, n_new, tk_s)
    a = _dsa(q.reshape(ns, n_new, ATT_Q), qi.reshape(ns, n_new, IDX_Q), wi.reshape(ns, n_new, V7X_LANES),
             cat(cache_k, kb, ATT_KV), cat(cache_v, vb, ATT_KV), cat(cache_kidx, kib, IDX_DIM), bias_s,
             tq=n_new, tk=tk_s, n_sel=min(TOPK_MAX, total // 4), causal=False, q_start=past, n_keys=total)
    y_s = dense_out(a.reshape(ms, ATT_Q), gate, ga, gb, hs, 128).reshape(ns, n_new, D_MODEL)

    return (y_p, y_s,
            kf.reshape(1, nb, seq, KV_HEADS, HEAD_DIM), vf.reshape(1, nb, seq, KV_HEADS, HEAD_DIM),
            kif.reshape(1, nb, seq, IDX_DIM),
            kfs.reshape(1, ns, n_new, KV_HEADS, HEAD_DIM), vfs.reshape(1, ns, n_new, KV_HEADS, HEAD_DIM),
            kifs.reshape(1, ns, n_new, IDX_DIM), vg.reshape(1, ns, n_new, D_MODEL))
```

```python
import functools
import math

import numpy as np
import jax
import jax.numpy as jnp
from jax import lax
from jax.experimental import pallas as pl
from jax.experimental.pallas import tpu as pltpu

D_MODEL = 1024
CHUNK = 64
N_HEADS = 16
HEAD_DIM = 64
KV_HEADS = 4
GROUP = N_HEADS // KV_HEADS
IDX_HEADS = 8
IDX_DIM = 64
TOPK_MAX = 256
GM_CHUNK = 128
GM_GROUPS = 4
GM_GROUP_DIM = D_MODEL // GM_GROUPS
D_FF = 2816
REL_BUCKETS = 32
REL_MAX_DIST = 128
LN_EPS = 1e-5
ATT_Q = N_HEADS * HEAD_DIM
ATT_KV = KV_HEADS * HEAD_DIM
IDX_Q = IDX_HEADS * IDX_DIM

V7X_LANES = 128
V7X_SUBLANES = 8
V7X_MXU_DIM = 256
V7X_VMEM_BYTES = 64 * 1024 * 1024

FF_BLOCK = V7X_MXU_DIM
NEG_LOGIT = -1e30
INT_MIN = -2 ** 31
F32 = jnp.float32
BF16 = jnp.bfloat16


def _vmem_limit(nbytes):
    return int(min(max(nbytes, 16 * 1024 * 1024), V7X_VMEM_BYTES - 6 * 1024 * 1024))


def _resident(shape):
    zeros = (0,) * len(shape)
    return pl.BlockSpec(shape, lambda *_: zeros, pipeline_mode=pl.Buffered(1))


def _layer_norm(x, g, b):
    mu = jnp.mean(x, axis=-1, keepdims=True)
    xc = x - mu
    var = jnp.mean(xc * xc, axis=-1, keepdims=True)
    return xc * lax.rsqrt(var + LN_EPS) * g + b


def _swiglu(xb, wg_ref, wu_ref, wd_ref, act_scr):
    for c in range(D_FF // FF_BLOCK):
        cols = slice(c * FF_BLOCK, (c + 1) * FF_BLOCK)
        g = jnp.dot(xb, wg_ref[:, cols], preferred_element_type=F32)
        u = jnp.dot(xb, wu_ref[:, cols], preferred_element_type=F32)
        act_scr[:, cols] = (g * jax.nn.sigmoid(g) * u).astype(BF16)
    return jnp.dot(act_scr[...], wd_ref[...], preferred_element_type=F32)


def _sortable_key(score):
    score = jnp.where(score == 0.0, 0.0, score)
    bits = pltpu.bitcast(score, jnp.int32)
    return bits ^ ((bits >> 31) & 0x7FFFFFFF)


def _kth_largest_key(count_ge, shape, f_sel):
    def bit_body(i, carry):
        prefix, cnt_t = carry
        cand = prefix + lax.shift_left(jnp.int32(1), (31 - i).astype(jnp.int32))
        cnt = count_ge(cand)
        take = cnt >= f_sel
        return jnp.where(take, cand, prefix), jnp.where(take, cnt, cnt_t)

    return lax.fori_loop(0, 32, bit_body, (jnp.full(shape, INT_MIN, jnp.int32), jnp.zeros(shape, F32)))


def _ffn_ln_kernel(x_ref, wg_ref, wu_ref, wd_ref, g_ref, b_ref, o_ref, act_scr, *, alpha):
    x = x_ref[...]
    y = _swiglu(x.astype(BF16), wg_ref, wu_ref, wd_ref, act_scr)
    o_ref[...] = _layer_norm(alpha * x + 0.5 * y, g_ref[...], b_ref[...])


def _ffn_ln(x, wg, wu, wd, g, b, *, tm, alpha):
    m = x.shape[0]
    row = lambda i: (i, 0)
    weights = 3 * D_MODEL * D_FF * 2
    tiles = tm * D_MODEL * 4 * 4 + tm * D_FF * (2 + 3 * 4)
    return pl.pallas_call(
        functools.partial(_ffn_ln_kernel, alpha=alpha),
        out_shape=jax.ShapeDtypeStruct((m, D_MODEL), F32),
        grid=(m // tm,),
        in_specs=[pl.BlockSpec((tm, D_MODEL), row),
                  _resident((D_MODEL, D_FF)), _resident((D_MODEL, D_FF)), _resident((D_FF, D_MODEL)),
                  _resident((1, D_MODEL)), _resident((1, D_MODEL))],
        out_specs=pl.BlockSpec((tm, D_MODEL), row),
        scratch_shapes=[pltpu.VMEM((tm, D_FF), BF16)],
        compiler_params=pltpu.CompilerParams(dimension_semantics=("arbitrary",),
                                             vmem_limit_bytes=_vmem_limit(weights + tiles)),
        name="ffn_ln",
    )(x, wg, wu, wd, g, b)


def _mix_in_kernel(h_ref, wqkv_ref, widx_ref, wzg_ref, wga_ref, wgb_ref, gmg_ref, gmb_ref, ws_ref, bst_ref,
                   *out_refs, names, tm, gate_rows, kv_block):
    o = dict(zip(names, out_refs))
    hb = h_ref[...].astype(BF16)

    qkv = jnp.dot(hb, wqkv_ref[...], preferred_element_type=F32)
    q = qkv[:, :ATT_Q] * HEAD_DIM ** -0.5
    k = qkv[:, ATT_Q:ATT_Q + ATT_KV]
    v = qkv[:, ATT_Q + ATT_KV:]
    o["k_f32"][...] = k
    o["v_f32"][...] = v
    o["k"][...] = k.astype(BF16)

    idx = jnp.dot(hb, widx_ref[...], preferred_element_type=F32)
    qi = idx[:, :IDX_Q]
    ki = idx[:, IDX_Q:IDX_Q + IDX_DIM]
    wi = idx[:, IDX_Q + V7X_LANES:]
    o["ki_f32"][...] = ki
    o["ki"][...] = ki.astype(BF16)

    if kv_block is None:
        o["q"][...] = q.astype(BF16)
        o["v"][...] = v.astype(BF16)
        o["qi"][...] = qi.astype(BF16)
        o["wi"][...] = wi
    else:
        o["q"][...] = q.T.astype(BF16)
        o["qi"][...] = qi.T.astype(BF16)
        o["wi"][...] = wi.T[:IDX_HEADS]
        vt = v.T.astype(BF16)
        for c in range(tm // kv_block):
            o["v"][c] = vt[:, c * kv_block:(c + 1) * kv_block]

    o["ga"][...] = jnp.dot(hb, wga_ref[...], preferred_element_type=F32)
    o["gb"][...] = jnp.dot(hb, wgb_ref[...], preferred_element_type=F32)

    zg = jax.nn.gelu(jnp.dot(hb, wzg_ref[...], preferred_element_type=F32))
    u = zg[:, :D_MODEL]
    vg = _layer_norm(zg[:, D_MODEL:], gmg_ref[...], gmb_ref[...])
    if "vg" in o:
        o["vg"][...] = vg
    vgb = vg.astype(BF16)

    r = lax.broadcasted_iota(jnp.int32, (GM_CHUNK, GM_CHUNK), 0)
    c = lax.broadcasted_iota(jnp.int32, (GM_CHUNK, GM_CHUNK), 1)
    keep = (r >= c) & ((r // gate_rows) == (c // gate_rows))
    for grp in range(GM_GROUPS):
        cols = slice(grp * GM_GROUP_DIM, (grp + 1) * GM_GROUP_DIM)
        w = jnp.where(keep, ws_ref[grp], 0.0).astype(BF16)
        bias = bst_ref[:, grp:grp + 1]
        for ch in range(tm // GM_CHUNK):
            rows = slice(ch * GM_CHUNK, (ch + 1) * GM_CHUNK)
            s = jnp.dot(w, vgb[rows, cols], preferred_element_type=F32) + bias
            o["gate"][rows, cols] = (u[rows, cols] * s).astype(BF16)


def _mix_in(h, wqkv, widx, wzg, wga, wgb, gmg, gmb, ws, bst, *, tm, gate_rows, emit_vg, kv_block):
    m = h.shape[0]
    row = lambda i: (i, 0)
    col = lambda i: (0, i)
    rows = lambda n, dt: (jax.ShapeDtypeStruct((m, n), dt), pl.BlockSpec((tm, n), row))
    cols = lambda n, dt: (jax.ShapeDtypeStruct((n, m), dt), pl.BlockSpec((n, tm), col))
    outs = {"k_f32": rows(ATT_KV, F32), "v_f32": rows(ATT_KV, F32), "ki_f32": rows(IDX_DIM, F32),
            "k": rows(ATT_KV, BF16), "ki": rows(IDX_DIM, BF16),
            "gate": rows(D_MODEL, BF16), "ga": rows(D_MODEL, F32), "gb": rows(D_MODEL, F32)}
    if kv_block is None:
        outs.update(q=rows(ATT_Q, BF16), v=rows(ATT_KV, BF16), qi=rows(IDX_Q, BF16), wi=rows(V7X_LANES, F32))
    else:
        assert tm % kv_block == 0
        outs.update(q=cols(ATT_Q, BF16), qi=cols(IDX_Q, BF16), wi=cols(IDX_HEADS, F32),
                    v=(jax.ShapeDtypeStruct((m // kv_block, ATT_KV, kv_block), BF16),
                       pl.BlockSpec((tm // kv_block, ATT_KV, kv_block), lambda i: (i, 0, 0))))
    if emit_vg:
        outs["vg"] = rows(D_MODEL, F32)
    names = tuple(outs)
    n_in = wqkv.shape[1] + widx.shape[1] + wzg.shape[1] + 2 * D_MODEL
    weights = D_MODEL * n_in * 2
    tiles = tm * (n_in * 4 * 3 + D_MODEL * 4 * 2)
    res = pl.pallas_call(
        functools.partial(_mix_in_kernel, names=names, tm=tm, gate_rows=gate_rows, kv_block=kv_block),
        out_shape=[outs[n][0] for n in names],
        grid=(m // tm,),
        in_specs=[pl.BlockSpec((tm, D_MODEL), row),
                  _resident(wqkv.shape), _resident(widx.shape), _resident(wzg.shape),
                  _resident(wga.shape), _resident(wgb.shape),
                  _resident((1, D_MODEL)), _resident((1, D_MODEL)),
                  _resident(ws.shape), _resident(bst.shape)],
        out_specs=[outs[n][1] for n in names],
        compiler_params=pltpu.CompilerParams(dimension_semantics=("arbitrary",),
                                             vmem_limit_bytes=_vmem_limit(weights + tiles)),
        name="mix_in",
    )(h, wqkv, widx, wzg, wga, wgb, gmg, gmb, ws, bst)
    return dict(zip(names, res))


def _bucket_thresholds():
    half = REL_BUCKETS // 2
    max_exact = half // 2
    n = np.arange(1, 2 * REL_MAX_DIST, dtype=np.float32)
    large = max_exact + (np.log(n / np.float32(max_exact)) / np.float32(math.log(REL_MAX_DIST / max_exact))
                         * np.float32(half - max_exact)).astype(np.int32)
    large = np.minimum(large, half - 1)
    thr = [int(np.argmax(large >= b)) + 1 for b in range(max_exact + 1, half)]
    return max_exact, half, thr


def _bias_kernel(table_ref, o_ref, *, shape, key_axis, tk):
    max_exact, half, thr = _bucket_thresholds()
    d = pl.program_id(0)
    h = pl.program_id(1)
    rel = (lax.broadcasted_iota(jnp.int32, shape, key_axis) - lax.broadcasted_iota(jnp.int32, shape, 1 - key_axis)
           + (d - 1) * tk)
    n = jnp.abs(rel)
    large = jnp.full(shape, max_exact, jnp.int32)
    for t in thr:
        large = large + jnp.where(n >= t, 1, 0)
    bucket = jnp.where(rel > 0, half, 0) + jnp.where(n < max_exact, n, large)
    acc = jnp.zeros(shape, F32)
    for b in range(REL_BUCKETS):
        acc = acc + jnp.where(bucket == b, table_ref[b, h], 0.0)
    o_ref[...] = acc - table_ref[half - 1, h]


def _bias_tiles(rel_table, tq, tk, key_axis):
    shape = (tq, tk) if key_axis == 1 else (tk, tq)
    return pl.pallas_call(
        functools.partial(_bias_kernel, shape=shape, key_axis=key_axis, tk=tk),
        out_shape=jax.ShapeDtypeStruct((2, N_HEADS) + shape, F32),
        grid=(2, N_HEADS),
        in_specs=[pl.BlockSpec(memory_space=pltpu.SMEM)],
        out_specs=pl.BlockSpec((None, None) + shape, lambda d, h: (d, h, 0, 0)),
        compiler_params=pltpu.CompilerParams(dimension_semantics=("arbitrary", "arbitrary")),
        name="rel_bias",
    )(rel_table)


def _dsa_cols_kernel(qt_ref, qit_ref, wit_ref, k_ref, vt_ref, ki_ref, bias_ref, o_ref,
                     keys_scr, m_scr, l_scr, acc_scr, sa_scr, sb_scr, *, tq, tk, n_sel):
    f_sel = float(n_sel)
    j = pl.program_id(1)
    nkb = j + 1
    pos = j * tq + lax.broadcasted_iota(jnp.int32, (1, tq), 1)
    n_vis = (pos // CHUNK + 1) * CHUNK

    def score_body(kb, carry):
        off = pl.multiple_of(kb * tk, tk)
        kit = ki_ref[pl.ds(off, tk), :]
        s = jnp.zeros((tk, tq), F32)
        for h in range(IDX_HEADS):
            d = jnp.dot(kit, qit_ref[h * IDX_DIM:(h + 1) * IDX_DIM, :], preferred_element_type=F32)
            s = s + wit_ref[h:h + 1, :] * jnp.maximum(d, 0.0)
        kpos = off + lax.broadcasted_iota(jnp.int32, (tk, tq), 0)
        keys_scr[kb] = jnp.where(kpos < n_vis, _sortable_key(s), INT_MIN)
        return carry

    lax.fori_loop(0, nkb, score_body, 0)

    def count_ge(cand):
        def body(kb, part):
            hit = jnp.where(keys_scr[kb] >= cand, 1.0, 0.0)
            return part + jnp.sum(hit.reshape(tk // V7X_SUBLANES, V7X_SUBLANES, tq), axis=0)

        part = lax.fori_loop(0, nkb, body, jnp.zeros((V7X_SUBLANES, tq), F32))
        return jnp.sum(part, axis=0, keepdims=True)

    thr, cnt_thr = _kth_largest_key(count_ge, (1, tq), f_sel)

    tie = (n_vis > n_sel) & (cnt_thr != f_sel)
    any_tie = jnp.max(jnp.where(tie, 1.0, 0.0)) > 0.0

    @pl.when(any_tie)
    def _():
        room = f_sel - count_ge(thr + 1)
        rr = lax.broadcasted_iota(jnp.int32, (tk, tk), 0)
        cc = lax.broadcasted_iota(jnp.int32, (tk, tk), 1)
        lower = jnp.where(rr >= cc, 1.0, 0.0).astype(BF16)

        def body(kb, seen):
            t = keys_scr[kb]
            eq = t == thr
            eqf = jnp.where(eq, 1.0, 0.0)
            rank = jnp.dot(lower, eqf.astype(BF16), preferred_element_type=F32) + seen
            keys_scr[kb] = jnp.where(eq & tie & (rank > room), thr - 1, t)
            return seen + jnp.sum(eqf, axis=0, keepdims=True)

        lax.fori_loop(0, nkb, body, jnp.zeros((1, tq), F32))

    thr_sel = jnp.maximum(thr, INT_MIN + 1)

    m_scr[...] = jnp.full(m_scr.shape, NEG_LOGIT, F32)
    l_scr[...] = jnp.zeros(l_scr.shape, F32)
    acc_scr[...] = jnp.zeros(acc_scr.shape, F32)

    def logits(kb, h):
        off = pl.multiple_of(kb * tk, tk)
        g = h // GROUP
        kg = k_ref[pl.ds(off, tk), g * HEAD_DIM:(g + 1) * HEAD_DIM]
        return jnp.dot(kg, qt_ref[h * HEAD_DIM:(h + 1) * HEAD_DIM, :], preferred_element_type=F32)

    def step(kb, cur_scr, nxt_scr):
        near = kb - (nkb - 2)

        @pl.when(near >= 0)
        def _():
            for h in range(N_HEADS):
                cur_scr[h] = cur_scr[h] + bias_ref[near, h]

        mask = keys_scr[kb] >= thr_sel
        kb_next = jnp.minimum(kb + 1, nkb - 1)
        for h in range(N_HEADS):
            g = h // GROUP
            if nxt_scr is not None:
                nxt_scr[h] = logits(kb_next, h)
            s = jnp.where(mask, cur_scr[h], NEG_LOGIT)
            m_old = m_scr[h]
            m_new = jnp.maximum(m_old, jnp.max(s, axis=0, keepdims=True))
            alpha = jnp.exp(m_old - m_new)
            p = jnp.exp(s - m_new)
            l_scr[h] = alpha * l_scr[h] + jnp.sum(p, axis=0, keepdims=True)
            vg = vt_ref[kb, g * HEAD_DIM:(g + 1) * HEAD_DIM, :]
            acc_scr[h] = alpha * acc_scr[h] + jnp.dot(vg, p.astype(BF16), preferred_element_type=F32)
            m_scr[h] = m_new

    for h in range(N_HEADS):
        sa_scr[h] = logits(0, h)

    def pair_body(t, carry):
        step(2 * t, sa_scr, sb_scr)
        step(2 * t + 1, sb_scr, sa_scr)
        return carry

    lax.fori_loop(0, nkb // 2, pair_body, 0)

    @pl.when(nkb % 2 == 1)
    def _():
        step(nkb - 1, sa_scr, None)

    for h in range(0, N_HEADS, 2):
        pair = jnp.concatenate([acc_scr[h] / l_scr[h], acc_scr[h + 1] / l_scr[h + 1]], axis=0)
        o_ref[:, h * HEAD_DIM:(h + 2) * HEAD_DIM] = pair.T.astype(BF16)


def _dsa_cols(qt, qit, wit, k, vt, ki, bias, *, nb, seq, tq, n_sel):
    tk = tq
    nq = seq // tq
    assert tk >= REL_MAX_DIST and seq % tq == 0
    qmap = lambda b, j: (0, b * nq + j)
    once = lambda shape, imap: pl.BlockSpec(shape, imap, pipeline_mode=pl.Buffered(1))
    scratch = [pltpu.VMEM((nq, tk, tq), jnp.int32),
               pltpu.VMEM((N_HEADS, 1, tq), F32),
               pltpu.VMEM((N_HEADS, 1, tq), F32),
               pltpu.VMEM((N_HEADS, HEAD_DIM, tq), F32),
               pltpu.VMEM((N_HEADS, tk, tq), F32),
               pltpu.VMEM((N_HEADS, tk, tq), F32)]
    est = (seq * (2 * ATT_KV + V7X_LANES) * 2 + seq * tq * 4 + 4 * N_HEADS * tq * tk * 4
           + 2 * tq * (2 * ATT_Q + IDX_Q) * 2 + N_HEADS * HEAD_DIM * tq * 4
           + IDX_HEADS * tq * tk * 4 * 2 + 8 * tq * tk * 4)
    return pl.pallas_call(
        functools.partial(_dsa_cols_kernel, tq=tq, tk=tk, n_sel=n_sel),
        out_shape=jax.ShapeDtypeStruct((nb, seq, ATT_Q), BF16),
        grid=(nb, nq),
        in_specs=[pl.BlockSpec((ATT_Q, tq), qmap), pl.BlockSpec((IDX_Q, tq), qmap),
                  pl.BlockSpec((IDX_HEADS, tq), qmap),
                  once((None, seq, ATT_KV), lambda b, j: (b, 0, 0)),
                  once((None, nq, ATT_KV, tk), lambda b, j: (b, 0, 0, 0)),
                  once((None, seq, IDX_DIM), lambda b, j: (b, 0, 0)),
                  _resident(bias.shape)],
        out_specs=pl.BlockSpec((None, tq, ATT_Q), lambda b, j: (b, j, 0)),
        scratch_shapes=scratch,
        compiler_params=pltpu.CompilerParams(dimension_semantics=("arbitrary", "arbitrary"),
                                             vmem_limit_bytes=_vmem_limit(est)),
        name="dsa_cols",
    )(qt, qit, wit, k, vt, ki, bias)


def _dsa_rows_kernel(q_ref, qi_ref, wi_ref, k_ref, v_ref, ki_ref, bias_ref, o_ref,
                     keys_scr, qstk_scr, qistk_scr, wib_scr, m_scr, l_scr, acc_scr,
                     *, tq, tk, n_sel, n_keys):
    f_sel = float(n_sel)
    nkb = -(-n_keys // tk)
    n_vis = jnp.full((tq, 1), n_keys, jnp.int32)

    for hh in range(N_HEADS):
        g, i = divmod(hh, GROUP)
        qstk_scr[g, i * tq:(i + 1) * tq, :] = q_ref[:, hh * HEAD_DIM:(hh + 1) * HEAD_DIM]
    for h in range(IDX_HEADS):
        qistk_scr[h * tq:(h + 1) * tq, :] = qi_ref[:, h * IDX_DIM:(h + 1) * IDX_DIM]
        wib_scr[h] = jnp.broadcast_to(wi_ref[:, h:h + 1], (tq, tk))

    nt_dims = (((1,), (1,)), ((), ()))

    def score_body(kb, carry):
        off = pl.multiple_of(kb * tk, tk)
        d = lax.dot_general(qistk_scr[...], ki_ref[pl.ds(off, tk), :], nt_dims, preferred_element_type=F32)
        s = jnp.zeros((tq, tk), F32)
        for h in range(IDX_HEADS):
            s = s + wib_scr[h] * jnp.maximum(d[h * tq:(h + 1) * tq], 0.0)
        kpos = off + lax.broadcasted_iota(jnp.int32, (tq, tk), 1)
        keys_scr[kb] = jnp.where(kpos < n_vis, _sortable_key(s), INT_MIN)
        return carry

    lax.fori_loop(0, nkb, score_body, 0)

    def count_ge(cand):
        candb = jnp.broadcast_to(cand, (tq, V7X_LANES))

        def body(kb, part):
            t = keys_scr[kb]
            for c in range(tk // V7X_LANES):
                part = part + jnp.where(t[:, c * V7X_LANES:(c + 1) * V7X_LANES] >= candb, 1.0, 0.0)
            return part

        part = lax.fori_loop(0, nkb, body, jnp.zeros((tq, V7X_LANES), F32))
        return jnp.sum(part, axis=1, keepdims=True)

    thr, cnt_thr = _kth_largest_key(count_ge, (tq, 1), f_sel)

    tie = (n_vis > n_sel) & (cnt_thr != f_sel)
    any_tie = jnp.max(jnp.where(tie, 1.0, 0.0)) > 0.0

    @pl.when(any_tie)
    def _():
        room = f_sel - count_ge(thr + 1)
        rr = lax.broadcasted_iota(jnp.int32, (tk, tk), 0)
        cc = lax.broadcasted_iota(jnp.int32, (tk, tk), 1)
        upper = jnp.where(rr <= cc, 1.0, 0.0).astype(BF16)

        def body(kb, seen):
            t = keys_scr[kb]
            eq = t == thr
            eqf = jnp.where(eq, 1.0, 0.0)
            rank = jnp.dot(eqf.astype(BF16), upper, preferred_element_type=F32) + seen
            keys_scr[kb] = jnp.where(eq & tie & (rank > room), thr - 1, t)
            return seen + jnp.sum(eqf, axis=1, keepdims=True)

        lax.fori_loop(0, nkb, body, jnp.zeros((tq, 1), F32))

    thr_sel = jnp.maximum(thr, INT_MIN + 1)

    m_scr[...] = jnp.full(m_scr.shape, NEG_LOGIT, F32)
    l_scr[...] = jnp.zeros(l_scr.shape, F32)
    acc_scr[...] = jnp.zeros(acc_scr.shape, F32)

    def attend(kb, near):
        off = pl.multiple_of(kb * tk, tk)
        mask = (keys_scr[kb] >= thr_sel)[None]
        for g in range(KV_HEADS):
            cols = slice(g * HEAD_DIM, (g + 1) * HEAD_DIM)
            s = lax.dot_general(qstk_scr[g], k_ref[pl.ds(off, tk), cols], nt_dims, preferred_element_type=F32)
            s = s.reshape(GROUP, tq, tk)
            if near is not None:
                s = s + bias_ref[near, g * GROUP:(g + 1) * GROUP]
            s = jnp.where(mask, s, NEG_LOGIT)
            m_old = m_scr[g]
            m_new = jnp.maximum(m_old, jnp.max(s, axis=-1, keepdims=True))
            alpha = jnp.exp(m_old - m_new)
            p = jnp.exp(s - m_new)
            l_scr[g] = alpha * l_scr[g] + jnp.sum(p, axis=-1, keepdims=True)
            pv = jnp.dot(p.astype(BF16).reshape(GROUP * tq, tk), v_ref[pl.ds(off, tk), cols],
                         preferred_element_type=F32)
            acc_scr[g] = alpha.reshape(GROUP * tq, 1) * acc_scr[g] + pv
            m_scr[g] = m_new

    for kb in range(nkb):
        attend(kb, None if kb < nkb - 2 else kb - (nkb - 2))

    for hh in range(N_HEADS):
        g, i = divmod(hh, GROUP)
        o = acc_scr[g, i * tq:(i + 1) * tq, :] / l_scr[g, i]
        o_ref[:, hh * HEAD_DIM:(hh + 1) * HEAD_DIM] = o.astype(BF16)


def _dsa_rows(q, qi, wi, k, v, ki, bias, *, tq, tk, n_sel, q_start, n_keys):
    nb = q.shape[0]
    lp = k.shape[1]
    nkb = lp // tk
    qmap = lambda b: (b, 0, 0)
    assert tk >= REL_MAX_DIST and lp % tk == 0 and q_start == (nkb - 1) * tk and nkb == -(-n_keys // tk)
    scratch = [pltpu.VMEM((nkb, tq, tk), jnp.int32),
               pltpu.VMEM((KV_HEADS, GROUP * tq, HEAD_DIM), BF16),
               pltpu.VMEM((IDX_HEADS * tq, IDX_DIM), BF16),
               pltpu.VMEM((IDX_HEADS, tq, tk), F32),
               pltpu.VMEM((KV_HEADS, GROUP, tq, 1), F32),
               pltpu.VMEM((KV_HEADS, GROUP, tq, 1), F32),
               pltpu.VMEM((KV_HEADS, GROUP * tq, HEAD_DIM), F32)]
    est = (2 * lp * (2 * ATT_KV + V7X_LANES) * 2 + lp * tq * 4 + 2 * N_HEADS * tq * tk * 4
           + 3 * N_HEADS * tq * V7X_LANES * 4 + IDX_HEADS * tq * tk * 4 * 3 + 16 * tq * tk * 4 * 4)
    return pl.pallas_call(
        functools.partial(_dsa_rows_kernel, tq=tq, tk=tk, n_sel=n_sel, n_keys=n_keys),
        out_shape=jax.ShapeDtypeStruct((nb, tq, ATT_Q), BF16),
        grid=(nb,),
        in_specs=[pl.BlockSpec((None, tq, ATT_Q), qmap), pl.BlockSpec((None, tq, IDX_Q), qmap),
                  pl.BlockSpec((None, tq, V7X_LANES), qmap),
                  pl.BlockSpec((None, lp, ATT_KV), qmap), pl.BlockSpec((None, lp, ATT_KV), qmap),
                  pl.BlockSpec((None, lp, IDX_DIM), qmap),
                  _resident(bias.shape)],
        out_specs=pl.BlockSpec((None, tq, ATT_Q), qmap),
        scratch_shapes=scratch,
        compiler_params=pltpu.CompilerParams(dimension_semantics=("arbitrary",),
                                             vmem_limit_bytes=_vmem_limit(est)),
        name="dsa_rows",
    )(q, qi, wi, k, v, ki, bias)


def _merge_ffn_kernel(a_ref, gate_ref, ga_ref, gb_ref, h_ref, wa_ref, wb_ref, wo_ref, g2_ref, b2_ref,
                      wg_ref, wu_ref, wd_ref, g3_ref, b3_ref, o_ref, act_scr, *, alpha):
    br_a = jnp.dot(a_ref[...], wa_ref[...], preferred_element_type=F32)
    br_b = jnp.dot(gate_ref[...], wb_ref[...], preferred_element_type=F32)
    merge = jax.nn.sigmoid(ga_ref[...]) * br_a + jax.nn.sigmoid(gb_ref[...]) * br_b
    mixed = jnp.dot(merge.astype(BF16), wo_ref[...], preferred_element_type=F32)
    h2 = _layer_norm(alpha * h_ref[...] + mixed, g2_ref[...], b2_ref[...])
    y = _swiglu(h2.astype(BF16), wg_ref, wu_ref, wd_ref, act_scr)
    o_ref[...] = _layer_norm(alpha * h2 + 0.5 * y, g3_ref[...], b3_ref[...])


def _merge_ffn(a, gate, ga, gb, h, wa, wb, wo, g2, b2, wg, wu, wd, g3, b3, *, tm, alpha):
    m = h.shape[0]
    row = lambda i: (i, 0)
    tile = lambda: pl.BlockSpec((tm, D_MODEL), row)
    vec = lambda: _resident((1, D_MODEL))
    sq = lambda: _resident((D_MODEL, D_MODEL))
    weights = (3 * D_MODEL * D_MODEL + 3 * D_MODEL * D_FF) * 2
    tiles = tm * D_MODEL * 4 * 12 + tm * D_FF * (2 + 3 * 4)
    return pl.pallas_call(
        functools.partial(_merge_ffn_kernel, alpha=alpha),
        out_shape=jax.ShapeDtypeStruct((m, D_MODEL), F32),
        grid=(m // tm,),
        in_specs=[tile(), tile(), tile(), tile(), tile(), sq(), sq(), sq(), vec(), vec(),
                  _resident((D_MODEL, D_FF)), _resident((D_MODEL, D_FF)), _resident((D_FF, D_MODEL)),
                  vec(), vec()],
        out_specs=tile(),
        scratch_shapes=[pltpu.VMEM((tm, D_FF), BF16)],
        compiler_params=pltpu.CompilerParams(dimension_semantics=("arbitrary",),
                                             vmem_limit_bytes=_vmem_limit(weights + tiles)),
        name="merge_ffn",
    )(a, gate, ga, gb, h, wa, wb, wo, g2, b2, wg, wu, wd, g3, b3)


def _split_w_in(w_in):
    o_k = ATT_Q
    o_qi = ATT_Q + 2 * ATT_KV
    o_ki = o_qi + IDX_Q
    o_wi = o_ki + IDX_DIM
    o_zg = o_wi + IDX_HEADS
    o_ga = o_zg + 2 * D_MODEL
    o_gb = o_ga + D_MODEL
    wb = w_in.astype(BF16)
    pad = lambda w, n: jnp.pad(w, ((0, 0), (0, n - w.shape[1])))
    widx = jnp.concatenate([wb[:, o_qi:o_ki], pad(wb[:, o_ki:o_wi], V7X_LANES),
                            pad(wb[:, o_wi:o_zg], V7X_LANES)], axis=1)
    return wb[:, :o_qi], widx, wb[:, o_zg:o_ga], wb[:, o_ga:o_gb], wb[:, o_gb:]


def kernel(x_prompt, x_sample, cache_k, cache_v, cache_kidx, rel_table, ln1_g, ln1_b, ffn1_wg, ffn1_wu, ffn1_wd, w_in, gm_ln_g, gm_ln_b, gm_ws, gm_bs, w_br_a, w_br_b, w_out, ln2_g, ln2_b, ffn2_wg, ffn2_wu, ffn2_wd, ln3_g, ln3_b):
    depth = ln1_g.shape[0]
    assert depth == 1, "single-layer step"
    alpha = (2 * depth) ** 0.25
    nb, seq, _ = x_prompt.shape
    ns, n_new, _ = x_sample.shape
    past = cache_k.shape[2]
    total = past + n_new
    l = 0
    vec = lambda p: p[l].reshape(1, D_MODEL)
    bf = lambda p: p[l].astype(BF16)

    w_qkv, w_idx, w_zg, w_ga, w_gb = _split_w_in(w_in[l])
    f1 = (bf(ffn1_wg), bf(ffn1_wu), bf(ffn1_wd))
    f2 = (bf(ffn2_wg), bf(ffn2_wu), bf(ffn2_wd))
    br = (bf(w_br_a), bf(w_br_b), bf(w_out))

    def dense_in(x, tm, gate_rows, emit_vg, kv_block):
        h = _ffn_ln(x, *f1, vec(ln1_g), vec(ln1_b), tm=tm, alpha=alpha)
        reps = GM_CHUNK // gate_rows
        ws = jnp.tile(gm_ws[l][:, :gate_rows, :gate_rows], (1, reps, reps))
        bst = jnp.tile(gm_bs[l][:, :gate_rows].T, (reps, 1))
        parts = _mix_in(h, w_qkv, w_idx, w_zg, w_ga, w_gb, vec(gm_ln_g), vec(gm_ln_b), ws, bst,
                        tm=tm, gate_rows=gate_rows, emit_vg=emit_vg, kv_block=kv_block)
        return h, parts

    def dense_out(a, p, h, tm):
        return _merge_ffn(a, p["gate"], p["ga"], p["gb"], h, *br, vec(ln2_g), vec(ln2_b), *f2,
                          vec(ln3_g), vec(ln3_b), tm=tm, alpha=alpha)

    mp = nb * seq
    tq = 256
    hp, pp = dense_in(x_prompt.reshape(mp, D_MODEL), 512, GM_CHUNK, False, tq)
    bias_p = _bias_tiles(rel_table, tq, tq, 0)
    a = _dsa_cols(pp["q"], pp["qi"], pp["wi"], pp["k"].reshape(nb, seq, ATT_KV),
                  pp["v"].reshape(nb, seq // tq, ATT_KV, tq), pp["ki"].reshape(nb, seq, IDX_DIM), bias_p,
                  nb=nb, seq=seq, tq=tq, n_sel=min(TOPK_MAX, seq // 4))
    y_p = dense_out(a.reshape(mp, ATT_Q), pp, hp, 512).reshape(nb, seq, D_MODEL)

    ms = ns * n_new
    hs, ps = dense_in(x_sample.reshape(ms, D_MODEL), 128, n_new, True, None)
    tk_s = V7X_LANES
    lp = -(-total // tk_s) * tk_s
    cat = lambda c, new, w: jnp.pad(
        jnp.concatenate([c[l].reshape(ns, past, w).astype(BF16), new.reshape(ns, n_new, w)], axis=1),
        ((0, 0), (0, lp - total), (0, 0)))
    bias_s = _bias_tiles(rel_table, n_new, tk_s, 1)
    a = _dsa_rows(ps["q"].reshape(ns, n_new, ATT_Q), ps["qi"].reshape(ns, n_new, IDX_Q),
                  ps["wi"].reshape(ns, n_new, V7X_LANES),
                  cat(cache_k, ps["k"], ATT_KV), cat(cache_v, ps["v"], ATT_KV), cat(cache_kidx, ps["ki"], IDX_DIM),
                  bias_s, tq=n_new, tk=tk_s, n_sel=min(TOPK_MAX, total // 4), q_start=past, n_keys=total)
    y_s = dense_out(a.reshape(ms, ATT_Q), ps, hs, 128).reshape(ns, n_new, D_MODEL)

    kv5 = lambda x, b, t: x.reshape(1, b, t, KV_HEADS, HEAD_DIM)
    return (y_p, y_s,
            kv5(pp["k_f32"], nb, seq), kv5(pp["v_f32"], nb, seq), pp["ki_f32"].reshape(1, nb, seq, IDX_DIM),
            kv5(ps["k_f32"], ns, n_new), kv5(ps["v_f32"], ns, n_new), ps["ki_f32"].reshape(1, ns, n_new, IDX_DIM),
            ps["vg"].reshape(1, ns, n_new, D_MODEL))
```

```python
import functools
import math

import numpy as np
import jax
import jax.numpy as jnp
from jax import lax
from jax.experimental import pallas as pl
from jax.experimental.pallas import tpu as pltpu

D_MODEL = 1024
CHUNK = 64
N_HEADS = 16
HEAD_DIM = 64
KV_HEADS = 4
GROUP = N_HEADS // KV_HEADS
IDX_HEADS = 8
IDX_DIM = 64
TOPK_MAX = 256
GM_CHUNK = 128
GM_GROUPS = 4
GM_GROUP_DIM = D_MODEL // GM_GROUPS
D_FF = 2816
REL_BUCKETS = 32
REL_MAX_DIST = 128
LN_EPS = 1e-5
ATT_Q = N_HEADS * HEAD_DIM
ATT_KV = KV_HEADS * HEAD_DIM
IDX_Q = IDX_HEADS * IDX_DIM

V7X_LANES = 128
V7X_SUBLANES = 8
BF16_ROWS = 2 * V7X_SUBLANES
V7X_MXU_DIM = 256
V7X_VMEM_BYTES = 64 * 1024 * 1024

FF_BLOCK = V7X_MXU_DIM
NEG_LOGIT = -1e30
INT_MIN = -2 ** 31
F32 = jnp.float32
BF16 = jnp.bfloat16


def _vmem_limit(nbytes):
    return int(min(max(nbytes, 16 * 1024 * 1024), V7X_VMEM_BYTES - 6 * 1024 * 1024))


def _resident(shape):
    zeros = (0,) * len(shape)
    return pl.BlockSpec(shape, lambda *_: zeros, pipeline_mode=pl.Buffered(1))


def _layer_norm(x, g, b):
    mu = jnp.mean(x, axis=-1, keepdims=True)
    xc = x - mu
    var = jnp.mean(xc * xc, axis=-1, keepdims=True)
    return xc * lax.rsqrt(var + LN_EPS) * g + b


def _swiglu(xb, wg_ref, wu_ref, wd_ref, act_scr):
    for c in range(D_FF // FF_BLOCK):
        cols = slice(c * FF_BLOCK, (c + 1) * FF_BLOCK)
        g = jnp.dot(xb, wg_ref[:, cols], preferred_element_type=F32)
        u = jnp.dot(xb, wu_ref[:, cols], preferred_element_type=F32)
        act_scr[:, cols] = (g * jax.nn.sigmoid(g) * u).astype(BF16)
    return jnp.dot(act_scr[...], wd_ref[...], preferred_element_type=F32)


def _sortable_key(score):
    score = jnp.where(score == 0.0, 0.0, score)
    bits = pltpu.bitcast(score, jnp.int32)
    return bits ^ ((bits >> 31) & 0x7FFFFFFF)


def _kth_largest_key(count_ge, shape, f_sel):
    def bit_body(i, carry):
        prefix, cnt_t = carry
        cand = prefix + lax.shift_left(jnp.int32(1), jnp.int32(31) - i)
        cnt = count_ge(cand)
        take = cnt >= f_sel
        return jnp.where(take, cand, prefix), jnp.where(take, cnt, cnt_t)

    return lax.fori_loop(0, 32, bit_body, (jnp.full(shape, INT_MIN, jnp.int32), jnp.zeros(shape, F32)))


def _ffn_ln_kernel(x_ref, wg_ref, wu_ref, wd_ref, g_ref, b_ref, o_ref, act_scr, *, alpha):
    x = x_ref[...]
    y = _swiglu(x.astype(BF16), wg_ref, wu_ref, wd_ref, act_scr)
    o_ref[...] = _layer_norm(alpha * x + 0.5 * y, g_ref[...], b_ref[...])


def _ffn_ln(x, wg, wu, wd, g, b, *, tm, alpha):
    m = x.shape[0]
    row = lambda i: (i, 0)
    weights = 3 * D_MODEL * D_FF * 2
    tiles = tm * D_MODEL * 4 * 4 + tm * D_FF * (2 + 3 * 4)
    return pl.pallas_call(
        functools.partial(_ffn_ln_kernel, alpha=alpha),
        out_shape=jax.ShapeDtypeStruct((m, D_MODEL), F32),
        grid=(m // tm,),
        in_specs=[pl.BlockSpec((tm, D_MODEL), row),
                  _resident((D_MODEL, D_FF)), _resident((D_MODEL, D_FF)), _resident((D_FF, D_MODEL)),
                  _resident((1, D_MODEL)), _resident((1, D_MODEL))],
        out_specs=pl.BlockSpec((tm, D_MODEL), row),
        scratch_shapes=[pltpu.VMEM((tm, D_FF), BF16)],
        compiler_params=pltpu.CompilerParams(dimension_semantics=("arbitrary",),
                                             vmem_limit_bytes=_vmem_limit(weights + tiles)),
        name="ffn_ln",
    )(x, wg, wu, wd, g, b)


def _mix_in_kernel(h_ref, wqkv_ref, widx_ref, wzg_ref, wga_ref, wgb_ref, gmg_ref, gmb_ref, ws_ref, bst_ref,
                   *out_refs, names, tm, gate_rows, kv_block):
    o = dict(zip(names, out_refs))
    hb = h_ref[...].astype(BF16)

    qkv = jnp.dot(hb, wqkv_ref[...], preferred_element_type=F32)
    q = qkv[:, :ATT_Q] * HEAD_DIM ** -0.5
    k = qkv[:, ATT_Q:ATT_Q + ATT_KV]
    v = qkv[:, ATT_Q + ATT_KV:]
    o["k_f32"][...] = k
    o["v_f32"][...] = v
    o["k"][...] = k.astype(BF16)

    idx = jnp.dot(hb, widx_ref[...], preferred_element_type=F32)
    qi = idx[:, :IDX_Q]
    ki = idx[:, IDX_Q:IDX_Q + IDX_DIM]
    wi = idx[:, IDX_Q + V7X_LANES:]
    o["ki_f32"][...] = ki
    o["ki"][...] = ki.astype(BF16)

    if kv_block is None:
        o["q"][...] = q.astype(BF16)
        o["v"][...] = v.astype(BF16)
        o["qi"][...] = qi.astype(BF16)
        o["wi"][...] = wi
    else:
        o["q"][...] = q.T.astype(BF16)
        o["qi"][...] = qi.T.astype(BF16)
        o["wi"][...] = wi.T[:IDX_HEADS]
        vt = v.T.astype(BF16)
        for c in range(tm // kv_block):
            o["v"][c] = vt[:, c * kv_block:(c + 1) * kv_block]

    o["ga"][...] = jnp.dot(hb, wga_ref[...], preferred_element_type=F32)
    o["gb"][...] = jnp.dot(hb, wgb_ref[...], preferred_element_type=F32)

    zg = jax.nn.gelu(jnp.dot(hb, wzg_ref[...], preferred_element_type=F32))
    u = zg[:, :D_MODEL]
    vg = _layer_norm(zg[:, D_MODEL:], gmg_ref[...], gmb_ref[...])
    if "vg" in o:
        o["vg"][...] = vg
    vgb = vg.astype(BF16)

    r = lax.broadcasted_iota(jnp.int32, (GM_CHUNK, GM_CHUNK), 0)
    c = lax.broadcasted_iota(jnp.int32, (GM_CHUNK, GM_CHUNK), 1)
    keep = (r >= c) & ((r // gate_rows) == (c // gate_rows))
    for grp in range(GM_GROUPS):
        cols = slice(grp * GM_GROUP_DIM, (grp + 1) * GM_GROUP_DIM)
        w = jnp.where(keep, ws_ref[grp], 0.0).astype(BF16)
        bias = bst_ref[:, grp:grp + 1]
        for ch in range(tm // GM_CHUNK):
            rows = slice(ch * GM_CHUNK, (ch + 1) * GM_CHUNK)
            s = jnp.dot(w, vgb[rows, cols], preferred_element_type=F32) + bias
            o["gate"][rows, cols] = (u[rows, cols] * s).astype(BF16)


def _mix_in(h, wqkv, widx, wzg, wga, wgb, gmg, gmb, ws, bst, *, tm, gate_rows, emit_vg, kv_block):
    m = h.shape[0]
    row = lambda i: (i, 0)
    col = lambda i: (0, i)
    rows = lambda n, dt: (jax.ShapeDtypeStruct((m, n), dt), pl.BlockSpec((tm, n), row))
    cols = lambda n, dt: (jax.ShapeDtypeStruct((n, m), dt), pl.BlockSpec((n, tm), col))
    outs = {"k_f32": rows(ATT_KV, F32), "v_f32": rows(ATT_KV, F32), "ki_f32": rows(IDX_DIM, F32),
            "k": rows(ATT_KV, BF16), "ki": rows(IDX_DIM, BF16),
            "gate": rows(D_MODEL, BF16), "ga": rows(D_MODEL, F32), "gb": rows(D_MODEL, F32)}
    if kv_block is None:
        outs.update(q=rows(ATT_Q, BF16), v=rows(ATT_KV, BF16), qi=rows(IDX_Q, BF16), wi=rows(V7X_LANES, F32))
    else:
        assert tm % kv_block == 0
        outs.update(q=cols(ATT_Q, BF16), qi=cols(IDX_Q, BF16), wi=cols(IDX_HEADS, F32),
                    v=(jax.ShapeDtypeStruct((m // kv_block, ATT_KV, kv_block), BF16),
                       pl.BlockSpec((tm // kv_block, ATT_KV, kv_block), lambda i: (i, 0, 0))))
    if emit_vg:
        outs["vg"] = rows(D_MODEL, F32)
    names = tuple(outs)
    n_in = wqkv.shape[1] + widx.shape[1] + wzg.shape[1] + 2 * D_MODEL
    weights = D_MODEL * n_in * 2
    tiles = tm * (n_in * 4 * 3 + D_MODEL * 4 * 2)
    res = pl.pallas_call(
        functools.partial(_mix_in_kernel, names=names, tm=tm, gate_rows=gate_rows, kv_block=kv_block),
        out_shape=[outs[n][0] for n in names],
        grid=(m // tm,),
        in_specs=[pl.BlockSpec((tm, D_MODEL), row),
                  _resident(wqkv.shape), _resident(widx.shape), _resident(wzg.shape),
                  _resident(wga.shape), _resident(wgb.shape),
                  _resident((1, D_MODEL)), _resident((1, D_MODEL)),
                  _resident(ws.shape), _resident(bst.shape)],
        out_specs=[outs[n][1] for n in names],
        compiler_params=pltpu.CompilerParams(dimension_semantics=("arbitrary",),
                                             vmem_limit_bytes=_vmem_limit(weights + tiles)),
        name="mix_in",
    )(h, wqkv, widx, wzg, wga, wgb, gmg, gmb, ws, bst)
    return dict(zip(names, res))


def _bucket_thresholds():
    half = REL_BUCKETS // 2
    max_exact = half // 2
    n = np.arange(1, 2 * REL_MAX_DIST, dtype=np.float32)
    large = max_exact + (np.log(n / np.float32(max_exact)) / np.float32(math.log(REL_MAX_DIST / max_exact))
                         * np.float32(half - max_exact)).astype(np.int32)
    large = np.minimum(large, half - 1)
    thr = [int(np.argmax(large >= b)) + 1 for b in range(max_exact + 1, half)]
    return max_exact, half, thr


def _bias_kernel(table_ref, o_ref, *, shape, key_axis, tk):
    max_exact, half, thr = _bucket_thresholds()
    d = pl.program_id(0)
    h = pl.program_id(1)
    rel = (lax.broadcasted_iota(jnp.int32, shape, key_axis) - lax.broadcasted_iota(jnp.int32, shape, 1 - key_axis)
           + (d - 1) * tk)
    n = jnp.abs(rel)
    large = jnp.full(shape, max_exact, jnp.int32)
    for t in thr:
        large = large + jnp.where(n >= t, 1, 0)
    bucket = jnp.where(rel > 0, half, 0) + jnp.where(n < max_exact, n, large)
    acc = jnp.zeros(shape, F32)
    for b in range(REL_BUCKETS):
        acc = acc + jnp.where(bucket == b, table_ref[b, h], 0.0)
    o_ref[...] = acc - table_ref[half - 1, h]


def _bias_tiles(rel_table, tq, tk, key_axis):
    shape = (tq, tk) if key_axis == 1 else (tk, tq)
    return pl.pallas_call(
        functools.partial(_bias_kernel, shape=shape, key_axis=key_axis, tk=tk),
        out_shape=jax.ShapeDtypeStruct((2, N_HEADS) + shape, F32),
        grid=(2, N_HEADS),
        in_specs=[pl.BlockSpec(memory_space=pltpu.SMEM)],
        out_specs=pl.BlockSpec((None, None) + shape, lambda d, h: (d, h, 0, 0)),
        compiler_params=pltpu.CompilerParams(dimension_semantics=("arbitrary", "arbitrary")),
        name="rel_bias",
    )(rel_table)


def _dsa_cols_kernel(qt_ref, qit_ref, wit_ref, k_ref, vt_ref, ki_ref, bias_ref, o_ref,
                     keys_scr, m_scr, acc_scr, mb_scr, sa_scr, sb_scr, *, tq, tk, n_sel):
    f_sel = float(n_sel)
    j = pl.program_id(1)
    nkb = j + 1
    pos = j * tq + lax.broadcasted_iota(jnp.int32, (1, tq), 1)
    n_vis = (pos // CHUNK + 1) * CHUNK

    def score_body(kb, carry):
        off = pl.multiple_of(kb * tk, tk)
        kit = ki_ref[pl.ds(off, tk), :]
        s = jnp.zeros((tk, tq), F32)
        for h in range(IDX_HEADS):
            d = jnp.dot(kit, qit_ref[h * IDX_DIM:(h + 1) * IDX_DIM, :], preferred_element_type=F32)
            s = s + wit_ref[h:h + 1, :] * jnp.maximum(d, 0.0)
        kpos = off + lax.broadcasted_iota(jnp.int32, (tk, tq), 0)
        keys_scr[kb] = jnp.where(kpos < n_vis, _sortable_key(s), INT_MIN)
        return carry

    lax.fori_loop(0, nkb, score_body, 0)

    def count_ge(cand):
        def body(kb, part):
            hit = jnp.where(keys_scr[kb] >= cand, 1.0, 0.0)
            return part + jnp.sum(hit.reshape(tk // V7X_SUBLANES, V7X_SUBLANES, tq), axis=0)

        part = lax.fori_loop(0, nkb, body, jnp.zeros((V7X_SUBLANES, tq), F32))
        return jnp.sum(part, axis=0, keepdims=True)

    thr, cnt_thr = _kth_largest_key(count_ge, (1, tq), f_sel)

    tie = (n_vis > n_sel) & (cnt_thr != f_sel)
    any_tie = jnp.max(jnp.where(tie, 1.0, 0.0)) > 0.0

    @pl.when(any_tie)
    def _():
        room = f_sel - count_ge(thr + 1)
        rr = lax.broadcasted_iota(jnp.int32, (tk, tk), 0)
        cc = lax.broadcasted_iota(jnp.int32, (tk, tk), 1)
        lower = jnp.where(rr >= cc, 1.0, 0.0).astype(BF16)

        def body(kb, seen):
            t = keys_scr[kb]
            eq = t == thr
            eqf = jnp.where(eq, 1.0, 0.0)
            rank = jnp.dot(lower, eqf.astype(BF16), preferred_element_type=F32) + seen
            keys_scr[kb] = jnp.where(eq & tie & (rank > room), thr - 1, t)
            return seen + jnp.sum(eqf, axis=0, keepdims=True)

        lax.fori_loop(0, nkb, body, jnp.zeros((1, tq), F32))

    thr_sel = jnp.maximum(thr, INT_MIN + 1)

    m_scr[...] = jnp.full(m_scr.shape, NEG_LOGIT, F32)
    acc_scr[...] = jnp.zeros(acc_scr.shape, F32)
    ones_rows = jnp.ones((acc_scr.shape[1] - HEAD_DIM, tk), BF16)

    def logits(kb, h):
        off = pl.multiple_of(kb * tk, tk)
        g = h // GROUP
        kg = k_ref[pl.ds(off, tk), g * HEAD_DIM:(g + 1) * HEAD_DIM]
        return jnp.dot(kg, qt_ref[h * HEAD_DIM:(h + 1) * HEAD_DIM, :], preferred_element_type=F32)

    def step(kb, cur_scr, nxt_scr):
        near = kb - (nkb - 2)

        @pl.when(near >= 0)
        def _():
            for h in range(N_HEADS):
                cur_scr[h] = cur_scr[h] + bias_ref[near, h]

        mb_scr[...] = jnp.where(keys_scr[kb] >= thr_sel, 0.0, NEG_LOGIT)
        kb_next = jnp.minimum(kb + 1, nkb - 1)
        for h in range(N_HEADS):
            g = h // GROUP
            if nxt_scr is not None:
                nxt_scr[h] = logits(kb_next, h)
            ps, alphas = [], []
            for c in range(tq // V7X_LANES):
                lanes = slice(c * V7X_LANES, (c + 1) * V7X_LANES)
                s = cur_scr[h, :, lanes] + mb_scr[:, lanes]
                m_old = m_scr[h, :, lanes]
                m_new = jnp.maximum(m_old, jnp.max(s, axis=0, keepdims=True))
                alphas.append(jnp.exp(m_old - m_new))
                ps.append(jnp.exp(s - m_new).astype(BF16))
                m_scr[h, :, lanes] = m_new
            vg = jnp.concatenate([vt_ref[kb, g * HEAD_DIM:(g + 1) * HEAD_DIM, :], ones_rows], axis=0)
            pv = jnp.dot(vg, jnp.concatenate(ps, axis=1), preferred_element_type=F32)
            acc_scr[h] = jnp.concatenate(alphas, axis=1) * acc_scr[h] + pv

    for h in range(N_HEADS):
        sa_scr[h] = logits(0, h)

    def pair_body(t, carry):
        step(2 * t, sa_scr, sb_scr)
        step(2 * t + 1, sb_scr, sa_scr)
        return carry

    lax.fori_loop(0, nkb // 2, pair_body, 0)

    @pl.when(nkb % 2 == 1)
    def _():
        step(nkb - 1, sa_scr, None)

    def normalised(h):
        return acc_scr[h, :HEAD_DIM, :] / acc_scr[h, HEAD_DIM:HEAD_DIM + 1, :]

    for h in range(0, N_HEADS, 2):
        pair = jnp.concatenate([normalised(h), normalised(h + 1)], axis=0)
        o_ref[:, h * HEAD_DIM:(h + 2) * HEAD_DIM] = pair.T.astype(BF16)


def _dsa_cols(qt, qit, wit, k, vt, ki, bias, *, nb, seq, tq, n_sel):
    tk = tq
    nq = seq // tq
    assert tk >= REL_MAX_DIST and seq % tq == 0
    qmap = lambda b, j: (0, b * nq + j)
    once = lambda shape, imap: pl.BlockSpec(shape, imap, pipeline_mode=pl.Buffered(1))
    scratch = [pltpu.VMEM((nq, tk, tq), jnp.int32),
               pltpu.VMEM((N_HEADS, 1, tq), F32),
               pltpu.VMEM((N_HEADS, HEAD_DIM + BF16_ROWS, tq), F32),
               pltpu.VMEM((tk, tq), F32),
               pltpu.VMEM((N_HEADS, tk, tq), F32),
               pltpu.VMEM((N_HEADS, tk, tq), F32)]
    est = (seq * (2 * ATT_KV + V7X_LANES) * 2 + seq * tq * 4 + 4 * N_HEADS * tq * tk * 4
           + 2 * tq * (2 * ATT_Q + IDX_Q) * 2 + N_HEADS * HEAD_DIM * tq * 4
           + IDX_HEADS * tq * tk * 4 * 2 + 8 * tq * tk * 4)
    return pl.pallas_call(
        functools.partial(_dsa_cols_kernel, tq=tq, tk=tk, n_sel=n_sel),
        out_shape=jax.ShapeDtypeStruct((nb, seq, ATT_Q), BF16),
        grid=(nb, nq),
        in_specs=[pl.BlockSpec((ATT_Q, tq), qmap), pl.BlockSpec((IDX_Q, tq), qmap),
                  pl.BlockSpec((IDX_HEADS, tq), qmap),
                  once((None, seq, ATT_KV), lambda b, j: (b, 0, 0)),
                  once((None, nq, ATT_KV, tk), lambda b, j: (b, 0, 0, 0)),
                  once((None, seq, IDX_DIM), lambda b, j: (b, 0, 0)),
                  _resident(bias.shape)],
        out_specs=pl.BlockSpec((None, tq, ATT_Q), lambda b, j: (b, j, 0)),
        scratch_shapes=scratch,
        compiler_params=pltpu.CompilerParams(dimension_semantics=("arbitrary", "arbitrary"),
                                             vmem_limit_bytes=_vmem_limit(est)),
        name="dsa_cols",
    )(qt, qit, wit, k, vt, ki, bias)


def _dsa_rows_kernel(q_ref, qi_ref, wi_ref, k_ref, v_ref, ki_ref, bias_ref, o_ref,
                     keys_scr, qstk_scr, qistk_scr, wib_scr, m_scr, l_scr, acc_scr,
                     *, tq, tk, n_sel, n_keys):
    f_sel = float(n_sel)
    nkb = -(-n_keys // tk)
    n_vis = jnp.full((tq, 1), n_keys, jnp.int32)

    for hh in range(N_HEADS):
        g, i = divmod(hh, GROUP)
        qstk_scr[g, i * tq:(i + 1) * tq, :] = q_ref[:, hh * HEAD_DIM:(hh + 1) * HEAD_DIM]
    for h in range(IDX_HEADS):
        qistk_scr[h * tq:(h + 1) * tq, :] = qi_ref[:, h * IDX_DIM:(h + 1) * IDX_DIM]
        wib_scr[h] = jnp.broadcast_to(wi_ref[:, h:h + 1], (tq, tk))

    nt_dims = (((1,), (1,)), ((), ()))

    def score_body(kb, carry):
        off = pl.multiple_of(kb * tk, tk)
        d = lax.dot_general(qistk_scr[...], ki_ref[pl.ds(off, tk), :], nt_dims, preferred_element_type=F32)
        s = jnp.zeros((tq, tk), F32)
        for h in range(IDX_HEADS):
            s = s + wib_scr[h] * jnp.maximum(d[h * tq:(h + 1) * tq], 0.0)
        kpos = off + lax.broadcasted_iota(jnp.int32, (tq, tk), 1)
        keys_scr[kb] = jnp.where(kpos < n_vis, _sortable_key(s), INT_MIN)
        return carry

    lax.fori_loop(0, nkb, score_body, 0)

    def count_ge(cand):
        candb = jnp.broadcast_to(cand, (tq, V7X_LANES))

        def body(kb, part):
            t = keys_scr[kb]
            for c in range(tk // V7X_LANES):
                part = part + jnp.where(t[:, c * V7X_LANES:(c + 1) * V7X_LANES] >= candb, 1.0, 0.0)
            return part

        part = lax.fori_loop(0, nkb, body, jnp.zeros((tq, V7X_LANES), F32))
        return jnp.sum(part, axis=1, keepdims=True)

    thr, cnt_thr = _kth_largest_key(count_ge, (tq, 1), f_sel)

    tie = (n_vis > n_sel) & (cnt_thr != f_sel)
    any_tie = jnp.max(jnp.where(tie, 1.0, 0.0)) > 0.0

    @pl.when(any_tie)
    def _():
        room = f_sel - count_ge(thr + 1)
        rr = lax.broadcasted_iota(jnp.int32, (tk, tk), 0)
        cc = lax.broadcasted_iota(jnp.int32, (tk, tk), 1)
        upper = jnp.where(rr <= cc, 1.0, 0.0).astype(BF16)

        def body(kb, seen):
            t = keys_scr[kb]
            eq = t == thr
            eqf = jnp.where(eq, 1.0, 0.0)
            rank = jnp.dot(eqf.astype(BF16), upper, preferred_element_type=F32) + seen
            keys_scr[kb] = jnp.where(eq & tie & (rank > room), thr - 1, t)
            return seen + jnp.sum(eqf, axis=1, keepdims=True)

        lax.fori_loop(0, nkb, body, jnp.zeros((tq, 1), F32))

    thr_sel = jnp.maximum(thr, INT_MIN + 1)

    m_scr[...] = jnp.full(m_scr.shape, NEG_LOGIT, F32)
    l_scr[...] = jnp.zeros(l_scr.shape, F32)
    acc_scr[...] = jnp.zeros(acc_scr.shape, F32)

    def attend(kb, near):
        off = pl.multiple_of(kb * tk, tk)
        mask = (keys_scr[kb] >= thr_sel)[None]
        for g in range(KV_HEADS):
            cols = slice(g * HEAD_DIM, (g + 1) * HEAD_DIM)
            s = lax.dot_general(qstk_scr[g], k_ref[pl.ds(off, tk), cols], nt_dims, preferred_element_type=F32)
            s = s.reshape(GROUP, tq, tk)
            if near is not None:
                s = s + bias_ref[near, g * GROUP:(g + 1) * GROUP]
            s = jnp.where(mask, s, NEG_LOGIT)
            m_old = m_scr[g]
            m_new = jnp.maximum(m_old, jnp.max(s, axis=-1, keepdims=True))
            alpha = jnp.exp(m_old - m_new)
            p = jnp.exp(s - m_new)
            l_scr[g] = alpha * l_scr[g] + jnp.sum(p, axis=-1, keepdims=True)
            pv = jnp.dot(p.astype(BF16).reshape(GROUP * tq, tk), v_ref[pl.ds(off, tk), cols],
                         preferred_element_type=F32)
            acc_scr[g] = alpha.reshape(GROUP * tq, 1) * acc_scr[g] + pv
            m_scr[g] = m_new

    for kb in range(nkb):
        attend(kb, None if kb < nkb - 2 else kb - (nkb - 2))

    for hh in range(N_HEADS):
        g, i = divmod(hh, GROUP)
        o = acc_scr[g, i * tq:(i + 1) * tq, :] / l_scr[g, i]
        o_ref[:, hh * HEAD_DIM:(hh + 1) * HEAD_DIM] = o.astype(BF16)


def _dsa_rows(q, qi, wi, k, v, ki, bias, *, tq, tk, n_sel, q_start, n_keys):
    nb = q.shape[0]
    lp = k.shape[1]
    nkb = lp // tk
    qmap = lambda b: (b, 0, 0)
    assert tk >= REL_MAX_DIST and lp % tk == 0 and q_start == (nkb - 1) * tk and nkb == -(-n_keys // tk)
    scratch = [pltpu.VMEM((nkb, tq, tk), jnp.int32),
               pltpu.VMEM((KV_HEADS, GROUP * tq, HEAD_DIM), BF16),
               pltpu.VMEM((IDX_HEADS * tq, IDX_DIM), BF16),
               pltpu.VMEM((IDX_HEADS, tq, tk), F32),
               pltpu.VMEM((KV_HEADS, GROUP, tq, 1), F32),
               pltpu.VMEM((KV_HEADS, GROUP, tq, 1), F32),
               pltpu.VMEM((KV_HEADS, GROUP * tq, HEAD_DIM), F32)]
    est = (2 * lp * (2 * ATT_KV + V7X_LANES) * 2 + lp * tq * 4 + 2 * N_HEADS * tq * tk * 4
           + 3 * N_HEADS * tq * V7X_LANES * 4 + IDX_HEADS * tq * tk * 4 * 3 + 16 * tq * tk * 4 * 4)
    return pl.pallas_call(
        functools.partial(_dsa_rows_kernel, tq=tq, tk=tk, n_sel=n_sel, n_keys=n_keys),
        out_shape=jax.ShapeDtypeStruct((nb, tq, ATT_Q), BF16),
        grid=(nb,),
        in_specs=[pl.BlockSpec((None, tq, ATT_Q), qmap), pl.BlockSpec((None, tq, IDX_Q), qmap),
                  pl.BlockSpec((None, tq, V7X_LANES), qmap),
                  pl.BlockSpec((None, lp, ATT_KV), qmap), pl.BlockSpec((None, lp, ATT_KV), qmap),
                  pl.BlockSpec((None, lp, IDX_DIM), qmap),
                  _resident(bias.shape)],
        out_specs=pl.BlockSpec((None, tq, ATT_Q), qmap),
        scratch_shapes=scratch,
        compiler_params=pltpu.CompilerParams(dimension_semantics=("arbitrary",),
                                             vmem_limit_bytes=_vmem_limit(est)),
        name="dsa_rows",
    )(q, qi, wi, k, v, ki, bias)


def _merge_ffn_kernel(a_ref, gate_ref, ga_ref, gb_ref, h_ref, wa_ref, wb_ref, wo_ref, g2_ref, b2_ref,
                      wg_ref, wu_ref, wd_ref, g3_ref, b3_ref, o_ref, act_scr, *, alpha):
    br_a = jnp.dot(a_ref[...], wa_ref[...], preferred_element_type=F32)
    br_b = jnp.dot(gate_ref[...], wb_ref[...], preferred_element_type=F32)
    merge = jax.nn.sigmoid(ga_ref[...]) * br_a + jax.nn.sigmoid(gb_ref[...]) * br_b
    mixed = jnp.dot(merge.astype(BF16), wo_ref[...], preferred_element_type=F32)
    h2 = _layer_norm(alpha * h_ref[...] + mixed, g2_ref[...], b2_ref[...])
    y = _swiglu(h2.astype(BF16), wg_ref, wu_ref, wd_ref, act_scr)
    o_ref[...] = _layer_norm(alpha * h2 + 0.5 * y, g3_ref[...], b3_ref[...])


def _merge_ffn(a, gate, ga, gb, h, wa, wb, wo, g2, b2, wg, wu, wd, g3, b3, *, tm, alpha):
    m = h.shape[0]
    row = lambda i: (i, 0)
    tile = lambda: pl.BlockSpec((tm, D_MODEL), row)
    vec = lambda: _resident((1, D_MODEL))
    sq = lambda: _resident((D_MODEL, D_MODEL))
    weights = (3 * D_MODEL * D_MODEL + 3 * D_MODEL * D_FF) * 2
    tiles = tm * D_MODEL * 4 * 12 + tm * D_FF * (2 + 3 * 4)
    return pl.pallas_call(
        functools.partial(_merge_ffn_kernel, alpha=alpha),
        out_shape=jax.ShapeDtypeStruct((m, D_MODEL), F32),
        grid=(m // tm,),
        in_specs=[tile(), tile(), tile(), tile(), tile(), sq(), sq(), sq(), vec(), vec(),
                  _resident((D_MODEL, D_FF)), _resident((D_MODEL, D_FF)), _resident((D_FF, D_MODEL)),
                  vec(), vec()],
        out_specs=tile(),
        scratch_shapes=[pltpu.VMEM((tm, D_FF), BF16)],
        compiler_params=pltpu.CompilerParams(dimension_semantics=("arbitrary",),
                                             vmem_limit_bytes=_vmem_limit(weights + tiles)),
        name="merge_ffn",
    )(a, gate, ga, gb, h, wa, wb, wo, g2, b2, wg, wu, wd, g3, b3)


def _split_w_in(w_in):
    o_k = ATT_Q
    o_qi = ATT_Q + 2 * ATT_KV
    o_ki = o_qi + IDX_Q
    o_wi = o_ki + IDX_DIM
    o_zg = o_wi + IDX_HEADS
    o_ga = o_zg + 2 * D_MODEL
    o_gb = o_ga + D_MODEL
    wb = w_in.astype(BF16)
    pad = lambda w, n: jnp.pad(w, ((0, 0), (0, n - w.shape[1])))
    widx = jnp.concatenate([wb[:, o_qi:o_ki], pad(wb[:, o_ki:o_wi], V7X_LANES),
                            pad(wb[:, o_wi:o_zg], V7X_LANES)], axis=1)
    return wb[:, :o_qi], widx, wb[:, o_zg:o_ga], wb[:, o_ga:o_gb], wb[:, o_gb:]


def kernel(x_prompt, x_sample, cache_k, cache_v, cache_kidx, rel_table, ln1_g, ln1_b, ffn1_wg, ffn1_wu, ffn1_wd, w_in, gm_ln_g, gm_ln_b, gm_ws, gm_bs, w_br_a, w_br_b, w_out, ln2_g, ln2_b, ffn2_wg, ffn2_wu, ffn2_wd, ln3_g, ln3_b):
    depth = ln1_g.shape[0]
    assert depth == 1, "single-layer step"
    alpha = (2 * depth) ** 0.25
    nb, seq, _ = x_prompt.shape
    ns, n_new, _ = x_sample.shape
    past = cache_k.shape[2]
    total = past + n_new
    l = 0
    vec = lambda p: p[l].reshape(1, D_MODEL)
    bf = lambda p: p[l].astype(BF16)

    w_qkv, w_idx, w_zg, w_ga, w_gb = _split_w_in(w_in[l])
    f1 = (bf(ffn1_wg), bf(ffn1_wu), bf(ffn1_wd))
    f2 = (bf(ffn2_wg), bf(ffn2_wu), bf(ffn2_wd))
    br = (bf(w_br_a), bf(w_br_b), bf(w_out))

    def dense_in(x, tm, gate_rows, emit_vg, kv_block):
        h = _ffn_ln(x, *f1, vec(ln1_g), vec(ln1_b), tm=tm, alpha=alpha)
        reps = GM_CHUNK // gate_rows
        ws = jnp.tile(gm_ws[l][:, :gate_rows, :gate_rows], (1, reps, reps))
        bst = jnp.tile(gm_bs[l][:, :gate_rows].T, (reps, 1))
        parts = _mix_in(h, w_qkv, w_idx, w_zg, w_ga, w_gb, vec(gm_ln_g), vec(gm_ln_b), ws, bst,
                        tm=tm, gate_rows=gate_rows, emit_vg=emit_vg, kv_block=kv_block)
        return h, parts

    def dense_out(a, p, h, tm):
        return _merge_ffn(a, p["gate"], p["ga"], p["gb"], h, *br, vec(ln2_g), vec(ln2_b), *f2,
                          vec(ln3_g), vec(ln3_b), tm=tm, alpha=alpha)

    mp = nb * seq
    tq = 256
    hp, pp = dense_in(x_prompt.reshape(mp, D_MODEL), 512, GM_CHUNK, False, tq)
    bias_p = _bias_tiles(rel_table, tq, tq, 0)
    a = _dsa_cols(pp["q"], pp["qi"], pp["wi"], pp["k"].reshape(nb, seq, ATT_KV),
                  pp["v"].reshape(nb, seq // tq, ATT_KV, tq), pp["ki"].reshape(nb, seq, IDX_DIM), bias_p,
                  nb=nb, seq=seq, tq=tq, n_sel=min(TOPK_MAX, seq // 4))
    y_p = dense_out(a.reshape(mp, ATT_Q), pp, hp, 512).reshape(nb, seq, D_MODEL)

    ms = ns * n_new
    hs, ps = dense_in(x_sample.reshape(ms, D_MODEL), 128, n_new, True, None)
    tk_s = V7X_LANES
    lp = -(-total // tk_s) * tk_s
    cat = lambda c, new, w: jnp.pad(
        jnp.concatenate([c[l].reshape(ns, past, w).astype(BF16), new.reshape(ns, n_new, w)], axis=1),
        ((0, 0), (0, lp - total), (0, 0)))
    bias_s = _bias_tiles(rel_table, n_new, tk_s, 1)
    a = _dsa_rows(ps["q"].reshape(ns, n_new, ATT_Q), ps["qi"].reshape(ns, n_new, IDX_Q),
                  ps["wi"].reshape(ns, n_new, V7X_LANES),
                  cat(cache_k, ps["k"], ATT_KV), cat(cache_v, ps["v"], ATT_KV), cat(cache_kidx, ps["ki"], IDX_DIM),
                  bias_s, tq=n_new, tk=tk_s, n_sel=min(TOPK_MAX, total // 4), q_start=past, n_keys=total)
    y_s = dense_out(a.reshape(ms, ATT_Q), ps, hs, 128).reshape(ns, n_new, D_MODEL)

    kv5 = lambda x, b, t: x.reshape(1, b, t, KV_HEADS, HEAD_DIM)
    return (y_p, y_s,
            kv5(pp["k_f32"], nb, seq), kv5(pp["v_f32"], nb, seq), pp["ki_f32"].reshape(1, nb, seq, IDX_DIM),
            kv5(ps["k_f32"], ns, n_new), kv5(ps["v_f32"], ns, n_new), ps["ki_f32"].reshape(1, ns, n_new, IDX_DIM),
            ps["vg"].reshape(1, ns, n_new, D_MODEL))
```

```python
import functools
import math

import numpy as np
import jax
import jax.numpy as jnp
from jax import lax
from jax.experimental import pallas as pl
from jax.experimental.pallas import tpu as pltpu

D_MODEL = 1024
CHUNK = 64
N_HEADS = 16
HEAD_DIM = 64
KV_HEADS = 4
GROUP = N_HEADS // KV_HEADS
IDX_HEADS = 8
IDX_DIM = 64
TOPK_MAX = 256
GM_CHUNK = 128
GM_GROUPS = 4
GM_GROUP_DIM = D_MODEL // GM_GROUPS
D_FF = 2816
REL_BUCKETS = 32
REL_MAX_DIST = 128
LN_EPS = 1e-5
ATT_Q = N_HEADS * HEAD_DIM
ATT_KV = KV_HEADS * HEAD_DIM
IDX_Q = IDX_HEADS * IDX_DIM

V7X_LANES = 128
V7X_SUBLANES = 8
BF16_ROWS = 2 * V7X_SUBLANES
V7X_MXU_DIM = 256
V7X_VMEM_BYTES = 64 * 1024 * 1024

FF_BLOCK = V7X_MXU_DIM
LOG2_E = math.log2(math.e)
Q_SCALE = LOG2_E * HEAD_DIM ** -0.5
NEG_LOGIT = -1e30
INT_MIN = -2 ** 31
F32 = jnp.float32
BF16 = jnp.bfloat16


def _vmem_limit(nbytes):
    return int(min(max(nbytes, 16 * 1024 * 1024), V7X_VMEM_BYTES - 6 * 1024 * 1024))


def _resident(shape):
    zeros = (0,) * len(shape)
    return pl.BlockSpec(shape, lambda *_: zeros, pipeline_mode=pl.Buffered(1))


def _layer_norm(x, g, b):
    mu = jnp.mean(x, axis=-1, keepdims=True)
    xc = x - mu
    var = jnp.mean(xc * xc, axis=-1, keepdims=True)
    return xc * lax.rsqrt(var + LN_EPS) * g + b


def _swiglu(xb, wg_ref, wu_ref, wd_ref, act_scr):
    for c in range(D_FF // FF_BLOCK):
        cols = slice(c * FF_BLOCK, (c + 1) * FF_BLOCK)
        g = jnp.dot(xb, wg_ref[:, cols], preferred_element_type=F32)
        u = jnp.dot(xb, wu_ref[:, cols], preferred_element_type=F32)
        act_scr[:, cols] = (g * jax.nn.sigmoid(g) * u).astype(BF16)
    return jnp.dot(act_scr[...], wd_ref[...], preferred_element_type=F32)


def _sortable_key(score):
    score = jnp.where(score == 0.0, 0.0, score)
    bits = pltpu.bitcast(score, jnp.int32)
    return bits ^ ((bits >> 31) & 0x7FFFFFFF)


def _kth_largest_key(count_ge, f_sel, *, bits, lowest, cnt_lowest):
    def bit_body(i, carry):
        prefix, cnt_t = carry
        cand = prefix + lax.shift_left(jnp.int32(1), jnp.int32(bits - 1) - i)
        cnt = count_ge(cand)
        take = cnt >= f_sel
        return jnp.where(take, cand, prefix), jnp.where(take, cnt, cnt_t)

    return lax.fori_loop(0, bits, bit_body, (jnp.full(cnt_lowest.shape, lowest, jnp.int32), cnt_lowest))


def _ffn_ln_kernel(x_ref, wg_ref, wu_ref, wd_ref, g_ref, b_ref, o_ref, act_scr, *, alpha):
    x = x_ref[...]
    y = _swiglu(x.astype(BF16), wg_ref, wu_ref, wd_ref, act_scr)
    o_ref[...] = _layer_norm(alpha * x + 0.5 * y, g_ref[...], b_ref[...])


def _ffn_ln(x, wg, wu, wd, g, b, *, tm, alpha):
    m = x.shape[0]
    row = lambda i: (i, 0)
    weights = 3 * D_MODEL * D_FF * 2
    tiles = tm * D_MODEL * 4 * 4 + tm * D_FF * (2 + 3 * 4)
    return pl.pallas_call(
        functools.partial(_ffn_ln_kernel, alpha=alpha),
        out_shape=jax.ShapeDtypeStruct((m, D_MODEL), F32),
        grid=(m // tm,),
        in_specs=[pl.BlockSpec((tm, D_MODEL), row),
                  _resident((D_MODEL, D_FF)), _resident((D_MODEL, D_FF)), _resident((D_FF, D_MODEL)),
                  _resident((1, D_MODEL)), _resident((1, D_MODEL))],
        out_specs=pl.BlockSpec((tm, D_MODEL), row),
        scratch_shapes=[pltpu.VMEM((tm, D_FF), BF16)],
        compiler_params=pltpu.CompilerParams(dimension_semantics=("arbitrary",),
                                             vmem_limit_bytes=_vmem_limit(weights + tiles)),
        name="ffn_ln",
    )(x, wg, wu, wd, g, b)


def _mix_in_kernel(h_ref, wqkv_ref, widx_ref, wzg_ref, wga_ref, wgb_ref, gmg_ref, gmb_ref, ws_ref, bst_ref,
                   *out_refs, names, tm, gate_rows, kv_block):
    o = dict(zip(names, out_refs))
    hb = h_ref[...].astype(BF16)

    qkv = jnp.dot(hb, wqkv_ref[...], preferred_element_type=F32)
    q = qkv[:, :ATT_Q] * Q_SCALE
    k = qkv[:, ATT_Q:ATT_Q + ATT_KV]
    v = qkv[:, ATT_Q + ATT_KV:]
    o["k_f32"][...] = k
    o["v_f32"][...] = v
    o["k"][...] = k.astype(BF16)

    idx = jnp.dot(hb, widx_ref[...], preferred_element_type=F32)
    qi = idx[:, :IDX_Q]
    ki = idx[:, IDX_Q:IDX_Q + IDX_DIM]
    wi = idx[:, IDX_Q + V7X_LANES:]
    o["ki_f32"][...] = ki
    o["ki"][...] = ki.astype(BF16)

    if kv_block is None:
        o["q"][...] = q.astype(BF16)
        o["v"][...] = v.astype(BF16)
        o["qi"][...] = qi.astype(BF16)
        o["wi"][...] = wi
    else:
        o["q"][...] = q.T.astype(BF16)
        o["qi"][...] = qi.T.astype(BF16)
        o["wi"][...] = wi.T[:IDX_HEADS]
        vt = v.T.astype(BF16)
        for c in range(tm // kv_block):
            o["v"][c] = vt[:, c * kv_block:(c + 1) * kv_block]

    o["ga"][...] = jnp.dot(hb, wga_ref[...], preferred_element_type=F32)
    o["gb"][...] = jnp.dot(hb, wgb_ref[...], preferred_element_type=F32)

    zg = jax.nn.gelu(jnp.dot(hb, wzg_ref[...], preferred_element_type=F32))
    u = zg[:, :D_MODEL]
    vg = _layer_norm(zg[:, D_MODEL:], gmg_ref[...], gmb_ref[...])
    if "vg" in o:
        o["vg"][...] = vg
    vgb = vg.astype(BF16)

    r = lax.broadcasted_iota(jnp.int32, (GM_CHUNK, GM_CHUNK), 0)
    c = lax.broadcasted_iota(jnp.int32, (GM_CHUNK, GM_CHUNK), 1)
    keep = (r >= c) & ((r // gate_rows) == (c // gate_rows))
    for grp in range(GM_GROUPS):
        cols = slice(grp * GM_GROUP_DIM, (grp + 1) * GM_GROUP_DIM)
        w = jnp.where(keep, ws_ref[grp], 0.0).astype(BF16)
        bias = bst_ref[:, grp:grp + 1]
        for ch in range(tm // GM_CHUNK):
            rows = slice(ch * GM_CHUNK, (ch + 1) * GM_CHUNK)
            s = jnp.dot(w, vgb[rows, cols], preferred_element_type=F32) + bias
            o["gate"][rows, cols] = (u[rows, cols] * s).astype(BF16)


def _mix_in(h, wqkv, widx, wzg, wga, wgb, gmg, gmb, ws, bst, *, tm, gate_rows, emit_vg, kv_block):
    m = h.shape[0]
    row = lambda i: (i, 0)
    col = lambda i: (0, i)
    rows = lambda n, dt: (jax.ShapeDtypeStruct((m, n), dt), pl.BlockSpec((tm, n), row))
    cols = lambda n, dt: (jax.ShapeDtypeStruct((n, m), dt), pl.BlockSpec((n, tm), col))
    outs = {"k_f32": rows(ATT_KV, F32), "v_f32": rows(ATT_KV, F32), "ki_f32": rows(IDX_DIM, F32),
            "k": rows(ATT_KV, BF16), "ki": rows(IDX_DIM, BF16),
            "gate": rows(D_MODEL, BF16), "ga": rows(D_MODEL, F32), "gb": rows(D_MODEL, F32)}
    if kv_block is None:
        outs.update(q=rows(ATT_Q, BF16), v=rows(ATT_KV, BF16), qi=rows(IDX_Q, BF16), wi=rows(V7X_LANES, F32))
    else:
        assert tm % kv_block == 0
        outs.update(q=cols(ATT_Q, BF16), qi=cols(IDX_Q, BF16), wi=cols(IDX_HEADS, F32),
                    v=(jax.ShapeDtypeStruct((m // kv_block, ATT_KV, kv_block), BF16),
                       pl.BlockSpec((tm // kv_block, ATT_KV, kv_block), lambda i: (i, 0, 0))))
    if emit_vg:
        outs["vg"] = rows(D_MODEL, F32)
    names = tuple(outs)
    n_in = wqkv.shape[1] + widx.shape[1] + wzg.shape[1] + 2 * D_MODEL
    weights = D_MODEL * n_in * 2
    tiles = tm * (n_in * 4 * 3 + D_MODEL * 4 * 2)
    res = pl.pallas_call(
        functools.partial(_mix_in_kernel, names=names, tm=tm, gate_rows=gate_rows, kv_block=kv_block),
        out_shape=[outs[n][0] for n in names],
        grid=(m // tm,),
        in_specs=[pl.BlockSpec((tm, D_MODEL), row),
                  _resident(wqkv.shape), _resident(widx.shape), _resident(wzg.shape),
                  _resident(wga.shape), _resident(wgb.shape),
                  _resident((1, D_MODEL)), _resident((1, D_MODEL)),
                  _resident(ws.shape), _resident(bst.shape)],
        out_specs=[outs[n][1] for n in names],
        compiler_params=pltpu.CompilerParams(dimension_semantics=("arbitrary",),
                                             vmem_limit_bytes=_vmem_limit(weights + tiles)),
        name="mix_in",
    )(h, wqkv, widx, wzg, wga, wgb, gmg, gmb, ws, bst)
    return dict(zip(names, res))


def _bucket_thresholds():
    half = REL_BUCKETS // 2
    max_exact = half // 2
    n = np.arange(1, 2 * REL_MAX_DIST, dtype=np.float32)
    large = max_exact + (np.log(n / np.float32(max_exact)) / np.float32(math.log(REL_MAX_DIST / max_exact))
                         * np.float32(half - max_exact)).astype(np.int32)
    large = np.minimum(large, half - 1)
    thr = [int(np.argmax(large >= b)) + 1 for b in range(max_exact + 1, half)]
    return max_exact, half, thr


def _bias_kernel(table_ref, o_ref, *, shape, key_axis, tk):
    max_exact, half, thr = _bucket_thresholds()
    d = pl.program_id(0)
    h = pl.program_id(1)
    rel = (lax.broadcasted_iota(jnp.int32, shape, key_axis) - lax.broadcasted_iota(jnp.int32, shape, 1 - key_axis)
           + (d - 1) * tk)
    n = jnp.abs(rel)
    large = jnp.full(shape, max_exact, jnp.int32)
    for t in thr:
        large = large + jnp.where(n >= t, 1, 0)
    bucket = jnp.where(rel > 0, half, 0) + jnp.where(n < max_exact, n, large)
    acc = jnp.zeros(shape, F32)
    for b in range(REL_BUCKETS):
        acc = acc + jnp.where(bucket == b, table_ref[b, h], 0.0)
    o_ref[...] = (acc - table_ref[half - 1, h]) * LOG2_E


def _bias_tiles(rel_table, tq, tk, key_axis):
    shape = (tq, tk) if key_axis == 1 else (tk, tq)
    return pl.pallas_call(
        functools.partial(_bias_kernel, shape=shape, key_axis=key_axis, tk=tk),
        out_shape=jax.ShapeDtypeStruct((2, N_HEADS) + shape, F32),
        grid=(2, N_HEADS),
        in_specs=[pl.BlockSpec(memory_space=pltpu.SMEM)],
        out_specs=pl.BlockSpec((None, None) + shape, lambda d, h: (d, h, 0, 0)),
        compiler_params=pltpu.CompilerParams(dimension_semantics=("arbitrary", "arbitrary")),
        name="rel_bias",
    )(rel_table)


def _dsa_cols_kernel(qt_ref, qit_ref, wit_ref, k_ref, vt_ref, ki_ref, bias_ref, o_ref,
                     hi_scr, d1_scr, d0_scr, m_scr, acc_scr, mb_scr, sa_scr, sb_scr, *, tq, tk, n_sel):
    f_sel = float(n_sel)
    i16 = jnp.int16
    j = pl.program_id(1)
    nkb = j + 1
    pos = j * tq + lax.broadcasted_iota(jnp.int32, (1, tq), 1)
    n_vis = (pos // CHUNK + 1) * CHUNK
    need_sel = n_vis > n_sel

    def score_body(kb, carry):
        off = pl.multiple_of(kb * tk, tk)
        kit = ki_ref[pl.ds(off, tk), :]
        s = jnp.zeros((tk, tq), F32)
        for h in range(IDX_HEADS):
            d = jnp.dot(kit, qit_ref[h * IDX_DIM:(h + 1) * IDX_DIM, :], preferred_element_type=F32)
            s = s + wit_ref[h:h + 1, :] * jnp.maximum(d, 0.0)
        kpos = off + lax.broadcasted_iota(jnp.int32, (tk, tq), 0)
        key = jnp.where(kpos < n_vis, _sortable_key(s), INT_MIN)
        hi_scr[kb] = (key >> 16).astype(i16)
        d1_scr[kb] = ((key >> 8) & 0xFF).astype(i16)
        d0_scr[kb] = (key & 0xFF).astype(i16)
        return carry

    lax.fori_loop(0, nkb, score_body, 0)

    one_bf, zero_bf = jnp.ones((), BF16), jnp.zeros((), BF16)

    def counter(digit_scr):
        def count_ge(cand):
            cand16 = cand.astype(i16)

            def body(kb, part):
                hit = jnp.where(digit_scr[kb] >= cand16, one_bf, zero_bf)
                tiles = [hit[r * BF16_ROWS:(r + 1) * BF16_ROWS] for r in range(tk // BF16_ROWS)]
                while len(tiles) > 1:
                    tiles = [a + b for a, b in zip(tiles[::2], tiles[1::2])]
                return part + tiles[0].astype(F32)

            part = lax.fori_loop(0, nkb, body, jnp.zeros((BF16_ROWS, tq), F32))
            return jnp.sum(part, axis=0, keepdims=True)

        return count_ge

    def refine(dst_scr, upper_scr, upper_thr):
        thr16 = upper_thr.astype(i16)

        def body(kb, carry):
            up = upper_scr[kb]
            dst_scr[kb] = jnp.where(up > thr16, i16(256), jnp.where(up == thr16, dst_scr[kb], i16(-1)))
            return carry

        lax.fori_loop(0, nkb, body, 0)

    cnt_all = jnp.zeros((1, tq), F32)
    thr_hi, cnt = _kth_largest_key(counter(hi_scr), f_sel, bits=16, lowest=-2 ** 15, cnt_lowest=cnt_all)
    thr_hi = jnp.where(need_sel, thr_hi, -2 ** 15)
    refine(d1_scr, hi_scr, thr_hi)
    thr_d1, cnt = _kth_largest_key(counter(d1_scr), f_sel, bits=8, lowest=0, cnt_lowest=cnt)
    thr_d1 = jnp.where(need_sel, thr_d1, 0)
    refine(d0_scr, d1_scr, thr_d1)
    count_d0 = counter(d0_scr)
    thr_d0, cnt = _kth_largest_key(count_d0, f_sel, bits=8, lowest=0, cnt_lowest=cnt)
    thr_d0 = jnp.where(need_sel, thr_d0, 1)

    tie = need_sel & (cnt != f_sel)
    any_tie = jnp.max(jnp.where(tie, 1.0, 0.0)) > 0.0

    @pl.when(any_tie)
    def _():
        room = f_sel - count_d0(thr_d0 + 1)
        rr = lax.broadcasted_iota(jnp.int32, (tk, tk), 0)
        cc = lax.broadcasted_iota(jnp.int32, (tk, tk), 1)
        lower = jnp.where(rr >= cc, 1.0, 0.0).astype(BF16)
        thr16 = thr_d0.astype(i16)
        demoted = (thr_d0 - 1).astype(i16)

        def body(kb, seen):
            t = d0_scr[kb]
            eq = t == thr16
            rank = jnp.dot(lower, jnp.where(eq, one_bf, zero_bf), preferred_element_type=F32) + seen
            drop = jnp.where(tie & (rank > room), 1.0, 0.0).astype(BF16) > zero_bf
            d0_scr[kb] = jnp.where(eq & drop, demoted, t)
            return rank[tk - 1:tk, :]

        lax.fori_loop(0, nkb, body, jnp.zeros((1, tq), F32))

    sel16 = thr_d0.astype(i16)

    m_scr[...] = jnp.full(m_scr.shape, NEG_LOGIT, F32)
    acc_scr[...] = jnp.zeros(acc_scr.shape, F32)
    ones_rows = jnp.ones((acc_scr.shape[1] - HEAD_DIM, tk), BF16)

    def masked_logits(kb, heads, dst_scr):
        off = pl.multiple_of(kb * tk, tk)
        mb_scr[...] = jnp.where(d0_scr[kb] >= sel16, zero_bf, jnp.asarray(NEG_LOGIT, BF16)).astype(F32)
        for h in heads:
            g = h // GROUP
            kg = k_ref[pl.ds(off, tk), g * HEAD_DIM:(g + 1) * HEAD_DIM]
            s = jnp.dot(kg, qt_ref[h * HEAD_DIM:(h + 1) * HEAD_DIM, :], preferred_element_type=F32)
            dst_scr[h] = s + mb_scr[...]
            yield h

    def step(kb, cur_scr, nxt_scr):
        near = kb - (nkb - 2)

        @pl.when(near >= 0)
        def _():
            for h in range(N_HEADS):
                cur_scr[h] = cur_scr[h] + bias_ref[near, h]

        heads = range(N_HEADS)
        producer = iter(heads) if nxt_scr is None else masked_logits(jnp.minimum(kb + 1, nkb - 1), heads, nxt_scr)
        for h in producer:
            g = h // GROUP
            ps, alphas = [], []
            for c in range(tq // V7X_LANES):
                lanes = slice(c * V7X_LANES, (c + 1) * V7X_LANES)
                m_old = m_scr[h, :, lanes]
                m_new = jnp.maximum(m_old, jnp.max(cur_scr[h, :, lanes], axis=0, keepdims=True))
                alphas.append(jnp.exp2(m_old - m_new))
                ps.append(jnp.exp2(cur_scr[h, :, lanes] - m_new).astype(BF16))
                m_scr[h, :, lanes] = m_new
            vg = jnp.concatenate([vt_ref[kb, g * HEAD_DIM:(g + 1) * HEAD_DIM, :], ones_rows], axis=0)
            pv = jnp.dot(vg, jnp.concatenate(ps, axis=1), preferred_element_type=F32)
            acc_scr[h] = jnp.concatenate(alphas, axis=1) * acc_scr[h] + pv

    for _ in masked_logits(0, range(N_HEADS), sa_scr):
        pass

    def pair_body(t, carry):
        step(2 * t, sa_scr, sb_scr)
        step(2 * t + 1, sb_scr, sa_scr)
        return carry

    lax.fori_loop(0, nkb // 2, pair_body, 0)

    @pl.when(nkb % 2 == 1)
    def _():
        step(nkb - 1, sa_scr, None)

    def normalised(h):
        return acc_scr[h, :HEAD_DIM, :] / acc_scr[h, HEAD_DIM:HEAD_DIM + 1, :]

    for h in range(0, N_HEADS, 2):
        pair = jnp.concatenate([normalised(h), normalised(h + 1)], axis=0)
        o_ref[:, h * HEAD_DIM:(h + 2) * HEAD_DIM] = pair.T.astype(BF16)


def _dsa_cols(qt, qit, wit, k, vt, ki, bias, *, nb, seq, tq, n_sel):
    tk = tq
    nq = seq // tq
    assert tk >= REL_MAX_DIST and seq % tq == 0
    qmap = lambda b, j: (0, b * nq + j)
    once = lambda shape, imap: pl.BlockSpec(shape, imap, pipeline_mode=pl.Buffered(1))
    scratch = [pltpu.VMEM((nq, tk, tq), jnp.int16),
               pltpu.VMEM((nq, tk, tq), jnp.int16),
               pltpu.VMEM((nq, tk, tq), jnp.int16),
               pltpu.VMEM((N_HEADS, 1, tq), F32),
               pltpu.VMEM((N_HEADS, HEAD_DIM + BF16_ROWS, tq), F32),
               pltpu.VMEM((tk, tq), F32),
               pltpu.VMEM((N_HEADS, tk, tq), F32),
               pltpu.VMEM((N_HEADS, tk, tq), F32)]
    est = (seq * (2 * ATT_KV + V7X_LANES) * 2 + 3 * seq * tq * 2 + 4 * N_HEADS * tq * tk * 4
           + 2 * tq * (2 * ATT_Q + IDX_Q) * 2 + N_HEADS * HEAD_DIM * tq * 4
           + IDX_HEADS * tq * tk * 4 * 2 + 8 * tq * tk * 4)
    return pl.pallas_call(
        functools.partial(_dsa_cols_kernel, tq=tq, tk=tk, n_sel=n_sel),
        out_shape=jax.ShapeDtypeStruct((nb, seq, ATT_Q), BF16),
        grid=(nb, nq),
        in_specs=[pl.BlockSpec((ATT_Q, tq), qmap), pl.BlockSpec((IDX_Q, tq), qmap),
                  pl.BlockSpec((IDX_HEADS, tq), qmap),
                  once((None, seq, ATT_KV), lambda b, j: (b, 0, 0)),
                  once((None, nq, ATT_KV, tk), lambda b, j: (b, 0, 0, 0)),
                  once((None, seq, IDX_DIM), lambda b, j: (b, 0, 0)),
                  _resident(bias.shape)],
        out_specs=pl.BlockSpec((None, tq, ATT_Q), lambda b, j: (b, j, 0)),
        scratch_shapes=scratch,
        compiler_params=pltpu.CompilerParams(dimension_semantics=("arbitrary", "arbitrary"),
                                             vmem_limit_bytes=_vmem_limit(est)),
        name="dsa_cols",
    )(qt, qit, wit, k, vt, ki, bias)


def _dsa_rows_kernel(q_ref, qi_ref, wi_ref, k_ref, v_ref, ki_ref, bias_ref, o_ref,
                     keys_scr, qstk_scr, qistk_scr, wib_scr, m_scr, l_scr, acc_scr,
                     *, tq, tk, n_sel, n_keys):
    f_sel = float(n_sel)
    nkb = -(-n_keys // tk)
    n_vis = jnp.full((tq, 1), n_keys, jnp.int32)

    for hh in range(N_HEADS):
        g, i = divmod(hh, GROUP)
        qstk_scr[g, i * tq:(i + 1) * tq, :] = q_ref[:, hh * HEAD_DIM:(hh + 1) * HEAD_DIM]
    for h in range(IDX_HEADS):
        qistk_scr[h * tq:(h + 1) * tq, :] = qi_ref[:, h * IDX_DIM:(h + 1) * IDX_DIM]
        wib_scr[h] = jnp.broadcast_to(wi_ref[:, h:h + 1], (tq, tk))

    nt_dims = (((1,), (1,)), ((), ()))

    def score_body(kb, carry):
        off = pl.multiple_of(kb * tk, tk)
        d = lax.dot_general(qistk_scr[...], ki_ref[pl.ds(off, tk), :], nt_dims, preferred_element_type=F32)
        s = jnp.zeros((tq, tk), F32)
        for h in range(IDX_HEADS):
            s = s + wib_scr[h] * jnp.maximum(d[h * tq:(h + 1) * tq], 0.0)
        kpos = off + lax.broadcasted_iota(jnp.int32, (tq, tk), 1)
        keys_scr[kb] = jnp.where(kpos < n_vis, _sortable_key(s), INT_MIN)
        return carry

    lax.fori_loop(0, nkb, score_body, 0)

    def count_ge(cand):
        candb = jnp.broadcast_to(cand, (tq, V7X_LANES))

        def body(kb, part):
            t = keys_scr[kb]
            for c in range(tk // V7X_LANES):
                part = part + jnp.where(t[:, c * V7X_LANES:(c + 1) * V7X_LANES] >= candb, 1.0, 0.0)
            return part

        part = lax.fori_loop(0, nkb, body, jnp.zeros((tq, V7X_LANES), F32))
        return jnp.sum(part, axis=1, keepdims=True)

    thr, cnt_thr = _kth_largest_key(count_ge, f_sel, bits=32, lowest=INT_MIN,
                                    cnt_lowest=jnp.zeros((tq, 1), F32))

    tie = (n_vis > n_sel) & (cnt_thr != f_sel)
    any_tie = jnp.max(jnp.where(tie, 1.0, 0.0)) > 0.0

    @pl.when(any_tie)
    def _():
        room = f_sel - count_ge(thr + 1)
        rr = lax.broadcasted_iota(jnp.int32, (tk, tk), 0)
        cc = lax.broadcasted_iota(jnp.int32, (tk, tk), 1)
        upper = jnp.where(rr <= cc, 1.0, 0.0).astype(BF16)

        def body(kb, seen):
            t = keys_scr[kb]
            eq = t == thr
            eqf = jnp.where(eq, 1.0, 0.0)
            rank = jnp.dot(eqf.astype(BF16), upper, preferred_element_type=F32) + seen
            keys_scr[kb] = jnp.where(eq & tie & (rank > room), thr - 1, t)
            return seen + jnp.sum(eqf, axis=1, keepdims=True)

        lax.fori_loop(0, nkb, body, jnp.zeros((tq, 1), F32))

    thr_sel = jnp.maximum(thr, INT_MIN + 1)

    m_scr[...] = jnp.full(m_scr.shape, NEG_LOGIT, F32)
    l_scr[...] = jnp.zeros(l_scr.shape, F32)
    acc_scr[...] = jnp.zeros(acc_scr.shape, F32)

    def attend(kb, near):
        off = pl.multiple_of(kb * tk, tk)
        mask = (keys_scr[kb] >= thr_sel)[None]
        for g in range(KV_HEADS):
            cols = slice(g * HEAD_DIM, (g + 1) * HEAD_DIM)
            s = lax.dot_general(qstk_scr[g], k_ref[pl.ds(off, tk), cols], nt_dims, preferred_element_type=F32)
            s = s.reshape(GROUP, tq, tk)
            if near is not None:
                s = s + bias_ref[near, g * GROUP:(g + 1) * GROUP]
            s = jnp.where(mask, s, NEG_LOGIT)
            m_old = m_scr[g]
            m_new = jnp.maximum(m_old, jnp.max(s, axis=-1, keepdims=True))
            alpha = jnp.exp2(m_old - m_new)
            p = jnp.exp2(s - m_new)
            l_scr[g] = alpha * l_scr[g] + jnp.sum(p, axis=-1, keepdims=True)
            pv = jnp.dot(p.astype(BF16).reshape(GROUP * tq, tk), v_ref[pl.ds(off, tk), cols],
                         preferred_element_type=F32)
            acc_scr[g] = alpha.reshape(GROUP * tq, 1) * acc_scr[g] + pv
            m_scr[g] = m_new

    for kb in range(nkb):
        attend(kb, None if kb < nkb - 2 else kb - (nkb - 2))

    for hh in range(N_HEADS):
        g, i = divmod(hh, GROUP)
        o = acc_scr[g, i * tq:(i + 1) * tq, :] / l_scr[g, i]
        o_ref[:, hh * HEAD_DIM:(hh + 1) * HEAD_DIM] = o.astype(BF16)


def _dsa_rows(q, qi, wi, k, v, ki, bias, *, tq, tk, n_sel, q_start, n_keys):
    nb = q.shape[0]
    lp = k.shape[1]
    nkb = lp // tk
    qmap = lambda b: (b, 0, 0)
    assert tk >= REL_MAX_DIST and lp % tk == 0 and q_start == (nkb - 1) * tk and nkb == -(-n_keys // tk)
    scratch = [pltpu.VMEM((nkb, tq, tk), jnp.int32),
               pltpu.VMEM((KV_HEADS, GROUP * tq, HEAD_DIM), BF16),
               pltpu.VMEM((IDX_HEADS * tq, IDX_DIM), BF16),
               pltpu.VMEM((IDX_HEADS, tq, tk), F32),
               pltpu.VMEM((KV_HEADS, GROUP, tq, 1), F32),
               pltpu.VMEM((KV_HEADS, GROUP, tq, 1), F32),
               pltpu.VMEM((KV_HEADS, GROUP * tq, HEAD_DIM), F32)]
    est = (2 * lp * (2 * ATT_KV + V7X_LANES) * 2 + lp * tq * 4 + 2 * N_HEADS * tq * tk * 4
           + 3 * N_HEADS * tq * V7X_LANES * 4 + IDX_HEADS * tq * tk * 4 * 3 + 16 * tq * tk * 4 * 4)
    return pl.pallas_call(
        functools.partial(_dsa_rows_kernel, tq=tq, tk=tk, n_sel=n_sel, n_keys=n_keys),
        out_shape=jax.ShapeDtypeStruct((nb, tq, ATT_Q), BF16),
        grid=(nb,),
        in_specs=[pl.BlockSpec((None, tq, ATT_Q), qmap), pl.BlockSpec((None, tq, IDX_Q), qmap),
                  pl.BlockSpec((None, tq, V7X_LANES), qmap),
                  pl.BlockSpec((None, lp, ATT_KV), qmap), pl.BlockSpec((None, lp, ATT_KV), qmap),
                  pl.BlockSpec((None, lp, IDX_DIM), qmap),
                  _resident(bias.shape)],
        out_specs=pl.BlockSpec((None, tq, ATT_Q), qmap),
        scratch_shapes=scratch,
        compiler_params=pltpu.CompilerParams(dimension_semantics=("arbitrary",),
                                             vmem_limit_bytes=_vmem_limit(est)),
        name="dsa_rows",
    )(q, qi, wi, k, v, ki, bias)


def _merge_ffn_kernel(a_ref, gate_ref, ga_ref, gb_ref, h_ref, wa_ref, wb_ref, wo_ref, g2_ref, b2_ref,
                      wg_ref, wu_ref, wd_ref, g3_ref, b3_ref, o_ref, act_scr, *, alpha):
    br_a = jnp.dot(a_ref[...], wa_ref[...], preferred_element_type=F32)
    br_b = jnp.dot(gate_ref[...], wb_ref[...], preferred_element_type=F32)
    merge = jax.nn.sigmoid(ga_ref[...]) * br_a + jax.nn.sigmoid(gb_ref[...]) * br_b
    mixed = jnp.dot(merge.astype(BF16), wo_ref[...], preferred_element_type=F32)
    h2 = _layer_norm(alpha * h_ref[...] + mixed, g2_ref[...], b2_ref[...])
    y = _swiglu(h2.astype(BF16), wg_ref, wu_ref, wd_ref, act_scr)
    o_ref[...] = _layer_norm(alpha * h2 + 0.5 * y, g3_ref[...], b3_ref[...])


def _merge_ffn(a, gate, ga, gb, h, wa, wb, wo, g2, b2, wg, wu, wd, g3, b3, *, tm, alpha):
    m = h.shape[0]
    row = lambda i: (i, 0)
    tile = lambda: pl.BlockSpec((tm, D_MODEL), row)
    vec = lambda: _resident((1, D_MODEL))
    sq = lambda: _resident((D_MODEL, D_MODEL))
    weights = (3 * D_MODEL * D_MODEL + 3 * D_MODEL * D_FF) * 2
    tiles = tm * D_MODEL * 4 * 12 + tm * D_FF * (2 + 3 * 4)
    return pl.pallas_call(
        functools.partial(_merge_ffn_kernel, alpha=alpha),
        out_shape=jax.ShapeDtypeStruct((m, D_MODEL), F32),
        grid=(m // tm,),
        in_specs=[tile(), tile(), tile(), tile(), tile(), sq(), sq(), sq(), vec(), vec(),
                  _resident((D_MODEL, D_FF)), _resident((D_MODEL, D_FF)), _resident((D_FF, D_MODEL)),
                  vec(), vec()],
        out_specs=tile(),
        scratch_shapes=[pltpu.VMEM((tm, D_FF), BF16)],
        compiler_params=pltpu.CompilerParams(dimension_semantics=("arbitrary",),
                                             vmem_limit_bytes=_vmem_limit(weights + tiles)),
        name="merge_ffn",
    )(a, gate, ga, gb, h, wa, wb, wo, g2, b2, wg, wu, wd, g3, b3)


def _split_w_in(w_in):
    o_k = ATT_Q
    o_qi = ATT_Q + 2 * ATT_KV
    o_ki = o_qi + IDX_Q
    o_wi = o_ki + IDX_DIM
    o_zg = o_wi + IDX_HEADS
    o_ga = o_zg + 2 * D_MODEL
    o_gb = o_ga + D_MODEL
    wb = w_in.astype(BF16)
    pad = lambda w, n: jnp.pad(w, ((0, 0), (0, n - w.shape[1])))
    widx = jnp.concatenate([wb[:, o_qi:o_ki], pad(wb[:, o_ki:o_wi], V7X_LANES),
                            pad(wb[:, o_wi:o_zg], V7X_LANES)], axis=1)
    return wb[:, :o_qi], widx, wb[:, o_zg:o_ga], wb[:, o_ga:o_gb], wb[:, o_gb:]


def kernel(x_prompt, x_sample, cache_k, cache_v, cache_kidx, rel_table, ln1_g, ln1_b, ffn1_wg, ffn1_wu, ffn1_wd, w_in, gm_ln_g, gm_ln_b, gm_ws, gm_bs, w_br_a, w_br_b, w_out, ln2_g, ln2_b, ffn2_wg, ffn2_wu, ffn2_wd, ln3_g, ln3_b):
    depth = ln1_g.shape[0]
    assert depth == 1, "single-layer step"
    alpha = (2 * depth) ** 0.25
    nb, seq, _ = x_prompt.shape
    ns, n_new, _ = x_sample.shape
    past = cache_k.shape[2]
    total = past + n_new
    l = 0
    vec = lambda p: p[l].reshape(1, D_MODEL)
    bf = lambda p: p[l].astype(BF16)

    w_qkv, w_idx, w_zg, w_ga, w_gb = _split_w_in(w_in[l])
    f1 = (bf(ffn1_wg), bf(ffn1_wu), bf(ffn1_wd))
    f2 = (bf(ffn2_wg), bf(ffn2_wu), bf(ffn2_wd))
    br = (bf(w_br_a), bf(w_br_b), bf(w_out))

    def dense_in(x, tm, gate_rows, emit_vg, kv_block):
        h = _ffn_ln(x, *f1, vec(ln1_g), vec(ln1_b), tm=tm, alpha=alpha)
        reps = GM_CHUNK // gate_rows
        ws = jnp.tile(gm_ws[l][:, :gate_rows, :gate_rows], (1, reps, reps))
        bst = jnp.tile(gm_bs[l][:, :gate_rows].T, (reps, 1))
        parts = _mix_in(h, w_qkv, w_idx, w_zg, w_ga, w_gb, vec(gm_ln_g), vec(gm_ln_b), ws, bst,
                        tm=tm, gate_rows=gate_rows, emit_vg=emit_vg, kv_block=kv_block)
        return h, parts

    def dense_out(a, p, h, tm):
        return _merge_ffn(a, p["gate"], p["ga"], p["gb"], h, *br, vec(ln2_g), vec(ln2_b), *f2,
                          vec(ln3_g), vec(ln3_b), tm=tm, alpha=alpha)

    mp = nb * seq
    tq = 256
    hp, pp = dense_in(x_prompt.reshape(mp, D_MODEL), 512, GM_CHUNK, False, tq)
    bias_p = _bias_tiles(rel_table, tq, tq, 0)
    a = _dsa_cols(pp["q"], pp["qi"], pp["wi"], pp["k"].reshape(nb, seq, ATT_KV),
                  pp["v"].reshape(nb, seq // tq, ATT_KV, tq), pp["ki"].reshape(nb, seq, IDX_DIM), bias_p,
                  nb=nb, seq=seq, tq=tq, n_sel=min(TOPK_MAX, seq // 4))
    y_p = dense_out(a.reshape(mp, ATT_Q), pp, hp, 512).reshape(nb, seq, D_MODEL)

    ms = ns * n_new
    hs, ps = dense_in(x_sample.reshape(ms, D_MODEL), 128, n_new, True, None)
    tk_s = V7X_LANES
    lp = -(-total // tk_s) * tk_s
    cat = lambda c, new, w: jnp.pad(
        jnp.concatenate([c[l].reshape(ns, past, w).astype(BF16), new.reshape(ns, n_new, w)], axis=1),
        ((0, 0), (0, lp - total), (0, 0)))
    bias_s = _bias_tiles(rel_table, n_new, tk_s, 1)
    a = _dsa_rows(ps["q"].reshape(ns, n_new, ATT_Q), ps["qi"].reshape(ns, n_new, IDX_Q),
                  ps["wi"].reshape(ns, n_new, V7X_LANES),
                  cat(cache_k, ps["k"], ATT_KV), cat(cache_v, ps["v"], ATT_KV), cat(cache_kidx, ps["ki"], IDX_DIM),
                  bias_s, tq=n_new, tk=tk_s, n_sel=min(TOPK_MAX, total // 4), q_start=past, n_keys=total)
    y_s = dense_out(a.reshape(ms, ATT_Q), ps, hs, 128).reshape(ns, n_new, D_MODEL)

    kv5 = lambda x, b, t: x.reshape(1, b, t, KV_HEADS, HEAD_DIM)
    return (y_p, y_s,
            kv5(pp["k_f32"], nb, seq), kv5(pp["v_f32"], nb, seq), pp["ki_f32"].reshape(1, nb, seq, IDX_DIM),
            kv5(ps["k_f32"], ns, n_new), kv5(ps["v_f32"], ns, n_new), ps["ki_f32"].reshape(1, ns, n_new, IDX_DIM),
            ps["vg"].reshape(1, ns, n_new, D_MODEL))
```

```python
import functools
import math

import numpy as np
import jax
import jax.numpy as jnp
from jax import lax
from jax.experimental import pallas as pl
from jax.experimental.pallas import tpu as pltpu

D_MODEL = 1024
CHUNK = 64
N_HEADS = 16
HEAD_DIM = 64
KV_HEADS = 4
GROUP = N_HEADS // KV_HEADS
IDX_HEADS = 8
IDX_DIM = 64
TOPK_MAX = 256
GM_CHUNK = 128
GM_GROUPS = 4
GM_GROUP_DIM = D_MODEL // GM_GROUPS
D_FF = 2816
REL_BUCKETS = 32
REL_MAX_DIST = 128
LN_EPS = 1e-5
ATT_Q = N_HEADS * HEAD_DIM
ATT_KV = KV_HEADS * HEAD_DIM
IDX_Q = IDX_HEADS * IDX_DIM

V7X_LANES = 128
V7X_SUBLANES = 8
BF16_ROWS = 2 * V7X_SUBLANES
FAR_LOOKAHEAD = 1
NEAR_LOOKAHEAD = 4
V7X_MXU_DIM = 256
V7X_VMEM_BYTES = 64 * 1024 * 1024

FF_BLOCK = V7X_MXU_DIM
LOG2_E = math.log2(math.e)
Q_SCALE = LOG2_E * HEAD_DIM ** -0.5
NEG_LOGIT = -1e30
INT_MIN = -2 ** 31
F32 = jnp.float32
BF16 = jnp.bfloat16


def _vmem_limit(nbytes):
    return int(min(max(nbytes, 16 * 1024 * 1024), V7X_VMEM_BYTES - 6 * 1024 * 1024))


def _resident(shape):
    zeros = (0,) * len(shape)
    return pl.BlockSpec(shape, lambda *_: zeros, pipeline_mode=pl.Buffered(1))


def _layer_norm(x, g, b):
    mu = jnp.mean(x, axis=-1, keepdims=True)
    xc = x - mu
    var = jnp.mean(xc * xc, axis=-1, keepdims=True)
    return xc * lax.rsqrt(var + LN_EPS) * g + b


def _swiglu(xb, wg_ref, wu_ref, wd_ref, act_scr):
    for c in range(D_FF // FF_BLOCK):
        cols = slice(c * FF_BLOCK, (c + 1) * FF_BLOCK)
        g = jnp.dot(xb, wg_ref[:, cols], preferred_element_type=F32)
        u = jnp.dot(xb, wu_ref[:, cols], preferred_element_type=F32)
        act_scr[:, cols] = (g * jax.nn.sigmoid(g) * u).astype(BF16)
    return jnp.dot(act_scr[...], wd_ref[...], preferred_element_type=F32)


def _sortable_key(score):
    score = jnp.where(score == 0.0, 0.0, score)
    bits = pltpu.bitcast(score, jnp.int32)
    return bits ^ ((bits >> 31) & 0x7FFFFFFF)


def _kth_largest_key(count_ge, f_sel, *, bits, lowest, cnt_lowest):
    def bit_body(i, carry):
        prefix, cnt_t = carry
        cand = prefix + lax.shift_left(jnp.int32(1), jnp.int32(bits - 1) - i)
        cnt = count_ge(cand)
        take = cnt >= f_sel
        return jnp.where(take, cand, prefix), jnp.where(take, cnt, cnt_t)

    return lax.fori_loop(0, bits, bit_body, (jnp.full(cnt_lowest.shape, lowest, jnp.int32), cnt_lowest))


def _ffn_ln_kernel(x_ref, wg_ref, wu_ref, wd_ref, g_ref, b_ref, o_ref, act_scr, *, alpha):
    x = x_ref[...]
    y = _swiglu(x.astype(BF16), wg_ref, wu_ref, wd_ref, act_scr)
    o_ref[...] = _layer_norm(alpha * x + 0.5 * y, g_ref[...], b_ref[...])


def _ffn_ln(x, wg, wu, wd, g, b, *, tm, alpha):
    m = x.shape[0]
    row = lambda i: (i, 0)
    weights = 3 * D_MODEL * D_FF * 2
    tiles = tm * D_MODEL * 4 * 4 + tm * D_FF * (2 + 3 * 4)
    return pl.pallas_call(
        functools.partial(_ffn_ln_kernel, alpha=alpha),
        out_shape=jax.ShapeDtypeStruct((m, D_MODEL), F32),
        grid=(m // tm,),
        in_specs=[pl.BlockSpec((tm, D_MODEL), row),
                  _resident((D_MODEL, D_FF)), _resident((D_MODEL, D_FF)), _resident((D_FF, D_MODEL)),
                  _resident((1, D_MODEL)), _resident((1, D_MODEL))],
        out_specs=pl.BlockSpec((tm, D_MODEL), row),
        scratch_shapes=[pltpu.VMEM((tm, D_FF), BF16)],
        compiler_params=pltpu.CompilerParams(dimension_semantics=("arbitrary",),
                                             vmem_limit_bytes=_vmem_limit(weights + tiles)),
        name="ffn_ln",
    )(x, wg, wu, wd, g, b)


def _mix_in_kernel(h_ref, wqkv_ref, widx_ref, wzg_ref, wga_ref, wgb_ref, gmg_ref, gmb_ref, ws_ref, bst_ref,
                   *out_refs, names, tm, gate_rows, kv_block):
    o = dict(zip(names, out_refs))
    hb = h_ref[...].astype(BF16)

    qkv = jnp.dot(hb, wqkv_ref[...], preferred_element_type=F32)
    q = qkv[:, :ATT_Q] * Q_SCALE
    k = qkv[:, ATT_Q:ATT_Q + ATT_KV]
    v = qkv[:, ATT_Q + ATT_KV:]
    o["k_f32"][...] = k
    o["v_f32"][...] = v
    o["k"][...] = k.astype(BF16)

    idx = jnp.dot(hb, widx_ref[...], preferred_element_type=F32)
    qi = idx[:, :IDX_Q]
    ki = idx[:, IDX_Q:IDX_Q + IDX_DIM]
    wi = idx[:, IDX_Q + V7X_LANES:]
    o["ki_f32"][...] = ki
    o["ki"][...] = ki.astype(BF16)

    if kv_block is None:
        o["q"][...] = q.astype(BF16)
        o["v"][...] = v.astype(BF16)
        o["qi"][...] = qi.astype(BF16)
        o["wi"][...] = wi
    else:
        o["q"][...] = q.T.astype(BF16)
        o["qi"][...] = qi.T.astype(BF16)
        o["wi"][...] = wi.T[:IDX_HEADS]
        vt = v.T.astype(BF16)
        for c in range(tm // kv_block):
            o["v"][c] = vt[:, c * kv_block:(c + 1) * kv_block]

    o["ga"][...] = jnp.dot(hb, wga_ref[...], preferred_element_type=F32)
    o["gb"][...] = jnp.dot(hb, wgb_ref[...], preferred_element_type=F32)

    zg = jax.nn.gelu(jnp.dot(hb, wzg_ref[...], preferred_element_type=F32))
    u = zg[:, :D_MODEL]
    vg = _layer_norm(zg[:, D_MODEL:], gmg_ref[...], gmb_ref[...])
    if "vg" in o:
        o["vg"][...] = vg
    vgb = vg.astype(BF16)

    r = lax.broadcasted_iota(jnp.int32, (GM_CHUNK, GM_CHUNK), 0)
    c = lax.broadcasted_iota(jnp.int32, (GM_CHUNK, GM_CHUNK), 1)
    keep = (r >= c) & ((r // gate_rows) == (c // gate_rows))
    for grp in range(GM_GROUPS):
        cols = slice(grp * GM_GROUP_DIM, (grp + 1) * GM_GROUP_DIM)
        w = jnp.where(keep, ws_ref[grp], 0.0).astype(BF16)
        bias = bst_ref[:, grp:grp + 1]
        for ch in range(tm // GM_CHUNK):
            rows = slice(ch * GM_CHUNK, (ch + 1) * GM_CHUNK)
            s = jnp.dot(w, vgb[rows, cols], preferred_element_type=F32) + bias
            o["gate"][rows, cols] = (u[rows, cols] * s).astype(BF16)


def _mix_in(h, wqkv, widx, wzg, wga, wgb, gmg, gmb, ws, bst, *, tm, gate_rows, emit_vg, kv_block):
    m = h.shape[0]
    row = lambda i: (i, 0)
    col = lambda i: (0, i)
    rows = lambda n, dt: (jax.ShapeDtypeStruct((m, n), dt), pl.BlockSpec((tm, n), row))
    cols = lambda n, dt: (jax.ShapeDtypeStruct((n, m), dt), pl.BlockSpec((n, tm), col))
    outs = {"k_f32": rows(ATT_KV, F32), "v_f32": rows(ATT_KV, F32), "ki_f32": rows(IDX_DIM, F32),
            "k": rows(ATT_KV, BF16), "ki": rows(IDX_DIM, BF16),
            "gate": rows(D_MODEL, BF16), "ga": rows(D_MODEL, F32), "gb": rows(D_MODEL, F32)}
    if kv_block is None:
        outs.update(q=rows(ATT_Q, BF16), v=rows(ATT_KV, BF16), qi=rows(IDX_Q, BF16), wi=rows(V7X_LANES, F32))
    else:
        assert tm % kv_block == 0
        outs.update(q=cols(ATT_Q, BF16), qi=cols(IDX_Q, BF16), wi=cols(IDX_HEADS, F32),
                    v=(jax.ShapeDtypeStruct((m // kv_block, ATT_KV, kv_block), BF16),
                       pl.BlockSpec((tm // kv_block, ATT_KV, kv_block), lambda i: (i, 0, 0))))
    if emit_vg:
        outs["vg"] = rows(D_MODEL, F32)
    names = tuple(outs)
    n_in = wqkv.shape[1] + widx.shape[1] + wzg.shape[1] + 2 * D_MODEL
    weights = D_MODEL * n_in * 2
    tiles = tm * (n_in * 4 * 3 + D_MODEL * 4 * 2)
    res = pl.pallas_call(
        functools.partial(_mix_in_kernel, names=names, tm=tm, gate_rows=gate_rows, kv_block=kv_block),
        out_shape=[outs[n][0] for n in names],
        grid=(m // tm,),
        in_specs=[pl.BlockSpec((tm, D_MODEL), row),
                  _resident(wqkv.shape), _resident(widx.shape), _resident(wzg.shape),
                  _resident(wga.shape), _resident(wgb.shape),
                  _resident((1, D_MODEL)), _resident((1, D_MODEL)),
                  _resident(ws.shape), _resident(bst.shape)],
        out_specs=[outs[n][1] for n in names],
        compiler_params=pltpu.CompilerParams(dimension_semantics=("arbitrary",),
                                             vmem_limit_bytes=_vmem_limit(weights + tiles)),
        name="mix_in",
    )(h, wqkv, widx, wzg, wga, wgb, gmg, gmb, ws, bst)
    return dict(zip(names, res))


def _bucket_thresholds():
    half = REL_BUCKETS // 2
    max_exact = half // 2
    n = np.arange(1, 2 * REL_MAX_DIST, dtype=np.float32)
    large = max_exact + (np.log(n / np.float32(max_exact)) / np.float32(math.log(REL_MAX_DIST / max_exact))
                         * np.float32(half - max_exact)).astype(np.int32)
    large = np.minimum(large, half - 1)
    thr = [int(np.argmax(large >= b)) + 1 for b in range(max_exact + 1, half)]
    return max_exact, half, thr


def _bias_kernel(table_ref, o_ref, *, shape, key_axis, tk):
    max_exact, half, thr = _bucket_thresholds()
    d = pl.program_id(0)
    h = pl.program_id(1)
    rel = (lax.broadcasted_iota(jnp.int32, shape, key_axis) - lax.broadcasted_iota(jnp.int32, shape, 1 - key_axis)
           + (d - 1) * tk)
    n = jnp.abs(rel)
    large = jnp.full(shape, max_exact, jnp.int32)
    for t in thr:
        large = large + jnp.where(n >= t, 1, 0)
    bucket = jnp.where(rel > 0, half, 0) + jnp.where(n < max_exact, n, large)
    acc = jnp.zeros(shape, F32)
    for b in range(REL_BUCKETS):
        acc = acc + jnp.where(bucket == b, table_ref[b, h], 0.0)
    o_ref[...] = (acc - table_ref[half - 1, h]) * LOG2_E


def _bias_tiles(rel_table, tq, tk, key_axis):
    shape = (tq, tk) if key_axis == 1 else (tk, tq)
    return pl.pallas_call(
        functools.partial(_bias_kernel, shape=shape, key_axis=key_axis, tk=tk),
        out_shape=jax.ShapeDtypeStruct((2, N_HEADS) + shape, F32),
        grid=(2, N_HEADS),
        in_specs=[pl.BlockSpec(memory_space=pltpu.SMEM)],
        out_specs=pl.BlockSpec((None, None) + shape, lambda d, h: (d, h, 0, 0)),
        compiler_params=pltpu.CompilerParams(dimension_semantics=("arbitrary", "arbitrary")),
        name="rel_bias",
    )(rel_table)


def _dsa_cols_kernel(qt_ref, qit_ref, wit_ref, k_ref, vt_ref, ki_ref, bias_ref, o_ref,
                     hi_scr, d1_scr, d0_scr, m_scr, acc_scr, mb_scr, sa_scr, sb_scr, *, tq, tk, n_sel):
    f_sel = float(n_sel)
    i16 = jnp.int16
    j = pl.program_id(1)
    nkb = j + 1
    pos = j * tq + lax.broadcasted_iota(jnp.int32, (1, tq), 1)
    n_vis = (pos // CHUNK + 1) * CHUNK
    need_sel = n_vis > n_sel

    def score_body(kb, carry):
        off = pl.multiple_of(kb * tk, tk)
        kit = ki_ref[pl.ds(off, tk), :]
        s = jnp.zeros((tk, tq), F32)
        for h in range(IDX_HEADS):
            d = jnp.dot(kit, qit_ref[h * IDX_DIM:(h + 1) * IDX_DIM, :], preferred_element_type=F32)
            s = s + wit_ref[h:h + 1, :] * jnp.maximum(d, 0.0)
        kpos = off + lax.broadcasted_iota(jnp.int32, (tk, tq), 0)
        key = jnp.where(kpos < n_vis, _sortable_key(s), INT_MIN)
        hi_scr[kb] = (key >> 16).astype(i16)
        d1_scr[kb] = ((key >> 8) & 0xFF).astype(i16)
        d0_scr[kb] = (key & 0xFF).astype(i16)
        return carry

    lax.fori_loop(0, nkb, score_body, 0)

    n_pairs = (nkb + 1) // 2

    @pl.when(nkb % 2 == 1)
    def _():
        hi_scr[nkb] = jnp.full((tk, tq), -2 ** 15, i16)
        d1_scr[nkb] = jnp.zeros((tk, tq), i16)
        d0_scr[nkb] = jnp.zeros((tk, tq), i16)

    one_bf, zero_bf = jnp.ones((), BF16), jnp.zeros((), BF16)

    def counter(digit_scr):
        def count_ge(cand):
            cand16 = cand.astype(i16)

            def body(t, part):
                tiles = []
                for kb in (2 * t, 2 * t + 1):
                    hit = jnp.where(digit_scr[kb] >= cand16, one_bf, zero_bf)
                    tiles += [hit[r * BF16_ROWS:(r + 1) * BF16_ROWS] for r in range(tk // BF16_ROWS)]
                while len(tiles) > 1:
                    tiles = [a + b for a, b in zip(tiles[::2], tiles[1::2])]
                return part + tiles[0].astype(F32)

            part = lax.fori_loop(0, n_pairs, body, jnp.zeros((BF16_ROWS, tq), F32))
            return jnp.sum(part, axis=0, keepdims=True)

        return count_ge

    def refine(dst_scr, upper_scr, upper_thr):
        thr16 = upper_thr.astype(i16)

        def body(t, carry):
            for kb in (2 * t, 2 * t + 1):
                up = upper_scr[kb]
                dst_scr[kb] = jnp.where(up > thr16, i16(256), jnp.where(up == thr16, dst_scr[kb], i16(-1)))
            return carry

        lax.fori_loop(0, n_pairs, body, 0)

    cnt_all = jnp.zeros((1, tq), F32)
    thr_hi, cnt = _kth_largest_key(counter(hi_scr), f_sel, bits=16, lowest=-2 ** 15, cnt_lowest=cnt_all)
    thr_hi = jnp.where(need_sel, thr_hi, -2 ** 15)
    refine(d1_scr, hi_scr, thr_hi)
    thr_d1, cnt = _kth_largest_key(counter(d1_scr), f_sel, bits=8, lowest=0, cnt_lowest=cnt)
    thr_d1 = jnp.where(need_sel, thr_d1, 0)
    refine(d0_scr, d1_scr, thr_d1)
    count_d0 = counter(d0_scr)
    thr_d0, cnt = _kth_largest_key(count_d0, f_sel, bits=8, lowest=0, cnt_lowest=cnt)
    thr_d0 = jnp.where(need_sel, thr_d0, 1)

    tie = need_sel & (cnt != f_sel)
    any_tie = jnp.max(jnp.where(tie, 1.0, 0.0)) > 0.0

    @pl.when(any_tie)
    def _():
        room = f_sel - count_d0(thr_d0 + 1)
        rr = lax.broadcasted_iota(jnp.int32, (tk, tk), 0)
        cc = lax.broadcasted_iota(jnp.int32, (tk, tk), 1)
        lower = jnp.where(rr >= cc, 1.0, 0.0).astype(BF16)
        thr16 = thr_d0.astype(i16)
        demoted = (thr_d0 - 1).astype(i16)

        def body(kb, seen):
            t = d0_scr[kb]
            eq = t == thr16
            rank = jnp.dot(lower, jnp.where(eq, one_bf, zero_bf), preferred_element_type=F32) + seen
            drop = jnp.where(tie & (rank > room), 1.0, 0.0).astype(BF16) > zero_bf
            d0_scr[kb] = jnp.where(eq & drop, demoted, t)
            return rank[tk - 1:tk, :]

        lax.fori_loop(0, nkb, body, jnp.zeros((1, tq), F32))

    sel16 = thr_d0.astype(i16)

    m_scr[...] = jnp.full(m_scr.shape, NEG_LOGIT, F32)
    acc_scr[...] = jnp.zeros(acc_scr.shape, F32)
    ones_rows = jnp.ones((acc_scr.shape[1] - HEAD_DIM, tk), BF16)

    def set_mask_tile(kb):
        mb_scr[...] = jnp.where(d0_scr[kb] >= sel16, zero_bf, jnp.asarray(NEG_LOGIT, BF16)).astype(F32)

    def produce(kb, h, dst_scr, near=None):
        off = pl.multiple_of(kb * tk, tk)
        g = h // GROUP
        kg = k_ref[pl.ds(off, tk), g * HEAD_DIM:(g + 1) * HEAD_DIM]
        s = jnp.dot(kg, qt_ref[h * HEAD_DIM:(h + 1) * HEAD_DIM, :], preferred_element_type=F32) + mb_scr[...]
        dst_scr[h] = s if near is None else s + bias_ref[near, h]

    def consume(kb, h, src_scr):
        g = h // GROUP
        ps, alphas = [], []
        for c in range(tq // V7X_LANES):
            lanes = slice(c * V7X_LANES, (c + 1) * V7X_LANES)
            m_old = m_scr[h, :, lanes]
            m_new = jnp.maximum(m_old, jnp.max(src_scr[h, :, lanes], axis=0, keepdims=True))
            alphas.append(jnp.exp2(m_old - m_new))
            ps.append(jnp.exp2(src_scr[h, :, lanes] - m_new).astype(BF16))
            m_scr[h, :, lanes] = m_new
        vg = jnp.concatenate([vt_ref[kb, g * HEAD_DIM:(g + 1) * HEAD_DIM, :], ones_rows], axis=0)
        pv = jnp.dot(vg, jnp.concatenate(ps, axis=1), preferred_element_type=F32)
        acc_scr[h] = jnp.concatenate(alphas, axis=1) * acc_scr[h] + pv

    n_far = jnp.maximum(nkb - 2, 0)

    def far_step(kb, cur_scr, nxt_scr):
        if nxt_scr is None:
            for h in range(N_HEADS):
                consume(kb, h, cur_scr)
            return
        kb_next = jnp.minimum(kb + 1, n_far - 1)

        @pl.when(kb_next >= 0)
        def _():
            set_mask_tile(kb_next)

        for h in range(-FAR_LOOKAHEAD, N_HEADS):
            if h + FAR_LOOKAHEAD < N_HEADS:
                produce(kb_next, h + FAR_LOOKAHEAD, nxt_scr)
            if h >= 0:
                consume(kb, h, cur_scr)

    @pl.when(n_far > 0)
    def _():
        set_mask_tile(0)
        for h in range(N_HEADS):
            produce(0, h, sa_scr)

    def pair_body(t, carry):
        far_step(2 * t, sa_scr, sb_scr)
        far_step(2 * t + 1, sb_scr, sa_scr)
        return carry

    lax.fori_loop(0, n_far // 2, pair_body, 0)

    @pl.when(n_far % 2 == 1)
    def _():
        far_step(n_far - 1, sa_scr, None)

    def near_body(kb, carry):
        near = kb - (nkb - 2)
        set_mask_tile(kb)
        for h in range(-NEAR_LOOKAHEAD, N_HEADS):
            if h + NEAR_LOOKAHEAD < N_HEADS:
                produce(kb, h + NEAR_LOOKAHEAD, sa_scr, near)
            if h >= 0:
                consume(kb, h, sa_scr)
        return carry

    lax.fori_loop(n_far, nkb, near_body, 0)

    def normalised(h):
        return acc_scr[h, :HEAD_DIM, :] / acc_scr[h, HEAD_DIM:HEAD_DIM + 1, :]

    for h in range(0, N_HEADS, 2):
        pair = jnp.concatenate([normalised(h), normalised(h + 1)], axis=0)
        o_ref[:, h * HEAD_DIM:(h + 2) * HEAD_DIM] = pair.T.astype(BF16)


def _dsa_cols(qt, qit, wit, k, vt, ki, bias, *, nb, seq, tq, n_sel):
    tk = tq
    nq = seq // tq
    assert tk >= REL_MAX_DIST and seq % tq == 0
    qmap = lambda b, j: (0, b * nq + j)
    once = lambda shape, imap: pl.BlockSpec(shape, imap, pipeline_mode=pl.Buffered(1))
    scratch = [pltpu.VMEM((nq, tk, tq), jnp.int16),
               pltpu.VMEM((nq, tk, tq), jnp.int16),
               pltpu.VMEM((nq, tk, tq), jnp.int16),
               pltpu.VMEM((N_HEADS, 1, tq), F32),
               pltpu.VMEM((N_HEADS, HEAD_DIM + BF16_ROWS, tq), F32),
               pltpu.VMEM((tk, tq), F32),
               pltpu.VMEM((N_HEADS, tk, tq), F32),
               pltpu.VMEM((N_HEADS, tk, tq), F32)]
    est = (seq * (2 * ATT_KV + V7X_LANES) * 2 + 3 * seq * tq * 2 + 4 * N_HEADS * tq * tk * 4
           + 2 * tq * (2 * ATT_Q + IDX_Q) * 2 + N_HEADS * HEAD_DIM * tq * 4
           + IDX_HEADS * tq * tk * 4 * 2 + 8 * tq * tk * 4)
    return pl.pallas_call(
        functools.partial(_dsa_cols_kernel, tq=tq, tk=tk, n_sel=n_sel),
        out_shape=jax.ShapeDtypeStruct((nb, seq, ATT_Q), BF16),
        grid=(nb, nq),
        in_specs=[pl.BlockSpec((ATT_Q, tq), qmap), pl.BlockSpec((IDX_Q, tq), qmap),
                  pl.BlockSpec((IDX_HEADS, tq), qmap),
                  once((None, seq, ATT_KV), lambda b, j: (b, 0, 0)),
                  once((None, nq, ATT_KV, tk), lambda b, j: (b, 0, 0, 0)),
                  once((None, seq, IDX_DIM), lambda b, j: (b, 0, 0)),
                  _resident(bias.shape)],
        out_specs=pl.BlockSpec((None, tq, ATT_Q), lambda b, j: (b, j, 0)),
        scratch_shapes=scratch,
        compiler_params=pltpu.CompilerParams(dimension_semantics=("arbitrary", "arbitrary"),
                                             vmem_limit_bytes=_vmem_limit(est)),
        name="dsa_cols",
    )(qt, qit, wit, k, vt, ki, bias)


def _dsa_rows_kernel(q_ref, qi_ref, wi_ref, k_ref, v_ref, ki_ref, bias_ref, o_ref,
                     keys_scr, qstk_scr, qistk_scr, wib_scr, *, tq, tk, n_sel, n_keys):
    f_sel = float(n_sel)
    nkb = -(-n_keys // tk)
    n_vis = jnp.full((tq, 1), n_keys, jnp.int32)

    for hh in range(N_HEADS):
        g, i = divmod(hh, GROUP)
        qstk_scr[g, i * tq:(i + 1) * tq, :] = q_ref[:, hh * HEAD_DIM:(hh + 1) * HEAD_DIM]
    for h in range(IDX_HEADS):
        qistk_scr[h * tq:(h + 1) * tq, :] = qi_ref[:, h * IDX_DIM:(h + 1) * IDX_DIM]
        wib_scr[h] = jnp.broadcast_to(wi_ref[:, h:h + 1], (tq, tk))

    nt_dims = (((1,), (1,)), ((), ()))

    def score_body(kb, carry):
        off = pl.multiple_of(kb * tk, tk)
        d = lax.dot_general(qistk_scr[...], ki_ref[pl.ds(off, tk), :], nt_dims, preferred_element_type=F32)
        s = jnp.zeros((tq, tk), F32)
        for h in range(IDX_HEADS):
            s = s + wib_scr[h] * jnp.maximum(d[h * tq:(h + 1) * tq], 0.0)
        kpos = off + lax.broadcasted_iota(jnp.int32, (tq, tk), 1)
        keys_scr[kb] = jnp.where(kpos < n_vis, _sortable_key(s), INT_MIN)
        return carry

    lax.fori_loop(0, nkb, score_body, 0)

    def count_ge(cand):
        candb = jnp.broadcast_to(cand, (tq, V7X_LANES))

        def body(kb, part):
            t = keys_scr[kb]
            for c in range(tk // V7X_LANES):
                part = part + jnp.where(t[:, c * V7X_LANES:(c + 1) * V7X_LANES] >= candb, 1.0, 0.0)
            return part

        part = lax.fori_loop(0, nkb, body, jnp.zeros((tq, V7X_LANES), F32))
        return jnp.sum(part, axis=1, keepdims=True)

    thr, cnt_thr = _kth_largest_key(count_ge, f_sel, bits=32, lowest=INT_MIN,
                                    cnt_lowest=jnp.zeros((tq, 1), F32))

    tie = (n_vis > n_sel) & (cnt_thr != f_sel)
    any_tie = jnp.max(jnp.where(tie, 1.0, 0.0)) > 0.0

    @pl.when(any_tie)
    def _():
        room = f_sel - count_ge(thr + 1)
        rr = lax.broadcasted_iota(jnp.int32, (tk, tk), 0)
        cc = lax.broadcasted_iota(jnp.int32, (tk, tk), 1)
        upper = jnp.where(rr <= cc, 1.0, 0.0).astype(BF16)

        def body(kb, seen):
            t = keys_scr[kb]
            eq = t == thr
            eqf = jnp.where(eq, 1.0, 0.0)
            rank = jnp.dot(eqf.astype(BF16), upper, preferred_element_type=F32) + seen
            keys_scr[kb] = jnp.where(eq & tie & (rank > room), thr - 1, t)
            return seen + jnp.sum(eqf, axis=1, keepdims=True)

        lax.fori_loop(0, nkb, body, jnp.zeros((tq, 1), F32))

    thr_sel = jnp.maximum(thr, INT_MIN + 1)

    lp = nkb * tk
    mask_add = jnp.concatenate([jnp.where(keys_scr[kb] >= thr_sel, 0.0, NEG_LOGIT) for kb in range(nkb)], axis=1)
    far = jnp.zeros((GROUP, tq, lp - 2 * tk), F32)
    for g in range(KV_HEADS):
        cols = slice(g * HEAD_DIM, (g + 1) * HEAD_DIM)
        heads = slice(g * GROUP, (g + 1) * GROUP)
        s = lax.dot_general(qstk_scr[g], k_ref[:, cols], nt_dims, preferred_element_type=F32)
        s = s.reshape(GROUP, tq, lp) + mask_add[None]
        s = s + jnp.concatenate([far, bias_ref[0, heads], bias_ref[1, heads]], axis=-1)
        p = jnp.exp2(s - jnp.max(s, axis=-1, keepdims=True))
        l = jnp.sum(p, axis=-1, keepdims=True)
        pv = jnp.dot(p.astype(BF16).reshape(GROUP * tq, lp), v_ref[:, cols], preferred_element_type=F32)
        o = pv.reshape(GROUP, tq, HEAD_DIM) / l
        for i in range(GROUP):
            hh = g * GROUP + i
            o_ref[:, hh * HEAD_DIM:(hh + 1) * HEAD_DIM] = o[i].astype(BF16)


def _dsa_rows(q, qi, wi, k, v, ki, bias, *, tq, tk, n_sel, q_start, n_keys):
    nb = q.shape[0]
    lp = k.shape[1]
    nkb = lp // tk
    qmap = lambda b: (b, 0, 0)
    assert tk >= REL_MAX_DIST and lp % tk == 0 and q_start == (nkb - 1) * tk and nkb == -(-n_keys // tk)
    scratch = [pltpu.VMEM((nkb, tq, tk), jnp.int32),
               pltpu.VMEM((KV_HEADS, GROUP * tq, HEAD_DIM), BF16),
               pltpu.VMEM((IDX_HEADS * tq, IDX_DIM), BF16),
               pltpu.VMEM((IDX_HEADS, tq, tk), F32)]
    est = (2 * lp * (2 * ATT_KV + V7X_LANES) * 2 + lp * tq * 4 + 2 * N_HEADS * tq * tk * 4
           + IDX_HEADS * tq * tk * 4 * 3 + 8 * N_HEADS * tq * lp * 4)
    return pl.pallas_call(
        functools.partial(_dsa_rows_kernel, tq=tq, tk=tk, n_sel=n_sel, n_keys=n_keys),
        out_shape=jax.ShapeDtypeStruct((nb, tq, ATT_Q), BF16),
        grid=(nb,),
        in_specs=[pl.BlockSpec((None, tq, ATT_Q), qmap), pl.BlockSpec((None, tq, IDX_Q), qmap),
                  pl.BlockSpec((None, tq, V7X_LANES), qmap),
                  pl.BlockSpec((None, lp, ATT_KV), qmap), pl.BlockSpec((None, lp, ATT_KV), qmap),
                  pl.BlockSpec((None, lp, IDX_DIM), qmap),
                  _resident(bias.shape)],
        out_specs=pl.BlockSpec((None, tq, ATT_Q), qmap),
        scratch_shapes=scratch,
        compiler_params=pltpu.CompilerParams(dimension_semantics=("arbitrary",),
                                             vmem_limit_bytes=_vmem_limit(est)),
        name="dsa_rows",
    )(q, qi, wi, k, v, ki, bias)


def _merge_ffn_kernel(a_ref, gate_ref, ga_ref, gb_ref, h_ref, wa_ref, wb_ref, wo_ref, g2_ref, b2_ref,
                      wg_ref, wu_ref, wd_ref, g3_ref, b3_ref, o_ref, act_scr, *, alpha):
    br_a = jnp.dot(a_ref[...], wa_ref[...], preferred_element_type=F32)
    br_b = jnp.dot(gate_ref[...], wb_ref[...], preferred_element_type=F32)
    merge = jax.nn.sigmoid(ga_ref[...]) * br_a + jax.nn.sigmoid(gb_ref[...]) * br_b
    mixed = jnp.dot(merge.astype(BF16), wo_ref[...], preferred_element_type=F32)
    h2 = _layer_norm(alpha * h_ref[...] + mixed, g2_ref[...], b2_ref[...])
    y = _swiglu(h2.astype(BF16), wg_ref, wu_ref, wd_ref, act_scr)
    o_ref[...] = _layer_norm(alpha * h2 + 0.5 * y, g3_ref[...], b3_ref[...])


def _merge_ffn(a, gate, ga, gb, h, wa, wb, wo, g2, b2, wg, wu, wd, g3, b3, *, tm, alpha):
    m = h.shape[0]
    row = lambda i: (i, 0)
    tile = lambda: pl.BlockSpec((tm, D_MODEL), row)
    vec = lambda: _resident((1, D_MODEL))
    sq = lambda: _resident((D_MODEL, D_MODEL))
    weights = (3 * D_MODEL * D_MODEL + 3 * D_MODEL * D_FF) * 2
    tiles = tm * D_MODEL * 4 * 12 + tm * D_FF * (2 + 3 * 4)
    return pl.pallas_call(
        functools.partial(_merge_ffn_kernel, alpha=alpha),
        out_shape=jax.ShapeDtypeStruct((m, D_MODEL), F32),
        grid=(m // tm,),
        in_specs=[tile(), tile(), tile(), tile(), tile(), sq(), sq(), sq(), vec(), vec(),
                  _resident((D_MODEL, D_FF)), _resident((D_MODEL, D_FF)), _resident((D_FF, D_MODEL)),
                  vec(), vec()],
        out_specs=tile(),
        scratch_shapes=[pltpu.VMEM((tm, D_FF), BF16)],
        compiler_params=pltpu.CompilerParams(dimension_semantics=("arbitrary",),
                                             vmem_limit_bytes=_vmem_limit(weights + tiles)),
        name="merge_ffn",
    )(a, gate, ga, gb, h, wa, wb, wo, g2, b2, wg, wu, wd, g3, b3)


def _split_w_in(w_in):
    o_k = ATT_Q
    o_qi = ATT_Q + 2 * ATT_KV
    o_ki = o_qi + IDX_Q
    o_wi = o_ki + IDX_DIM
    o_zg = o_wi + IDX_HEADS
    o_ga = o_zg + 2 * D_MODEL
    o_gb = o_ga + D_MODEL
    wb = w_in.astype(BF16)
    pad = lambda w, n: jnp.pad(w, ((0, 0), (0, n - w.shape[1])))
    widx = jnp.concatenate([wb[:, o_qi:o_ki], pad(wb[:, o_ki:o_wi], V7X_LANES),
                            pad(wb[:, o_wi:o_zg], V7X_LANES)], axis=1)
    return wb[:, :o_qi], widx, wb[:, o_zg:o_ga], wb[:, o_ga:o_gb], wb[:, o_gb:]


def kernel(x_prompt, x_sample, cache_k, cache_v, cache_kidx, rel_table, ln1_g, ln1_b, ffn1_wg, ffn1_wu, ffn1_wd, w_in, gm_ln_g, gm_ln_b, gm_ws, gm_bs, w_br_a, w_br_b, w_out, ln2_g, ln2_b, ffn2_wg, ffn2_wu, ffn2_wd, ln3_g, ln3_b):
    depth = ln1_g.shape[0]
    assert depth == 1, "single-layer step"
    alpha = (2 * depth) ** 0.25
    nb, seq, _ = x_prompt.shape
    ns, n_new, _ = x_sample.shape
    past = cache_k.shape[2]
    total = past + n_new
    l = 0
    vec = lambda p: p[l].reshape(1, D_MODEL)
    bf = lambda p: p[l].astype(BF16)

    w_qkv, w_idx, w_zg, w_ga, w_gb = _split_w_in(w_in[l])
    f1 = (bf(ffn1_wg), bf(ffn1_wu), bf(ffn1_wd))
    f2 = (bf(ffn2_wg), bf(ffn2_wu), bf(ffn2_wd))
    br = (bf(w_br_a), bf(w_br_b), bf(w_out))

    def dense_in(x, tm, gate_rows, emit_vg, kv_block):
        h = _ffn_ln(x, *f1, vec(ln1_g), vec(ln1_b), tm=tm, alpha=alpha)
        reps = GM_CHUNK // gate_rows
        ws = jnp.tile(gm_ws[l][:, :gate_rows, :gate_rows], (1, reps, reps))
        bst = jnp.tile(gm_bs[l][:, :gate_rows].T, (reps, 1))
        parts = _mix_in(h, w_qkv, w_idx, w_zg, w_ga, w_gb, vec(gm_ln_g), vec(gm_ln_b), ws, bst,
                        tm=tm, gate_rows=gate_rows, emit_vg=emit_vg, kv_block=kv_block)
        return h, parts

    def dense_out(a, p, h, tm):
        return _merge_ffn(a, p["gate"], p["ga"], p["gb"], h, *br, vec(ln2_g), vec(ln2_b), *f2,
                          vec(ln3_g), vec(ln3_b), tm=tm, alpha=alpha)

    mp = nb * seq
    tq = 256
    hp, pp = dense_in(x_prompt.reshape(mp, D_MODEL), 512, GM_CHUNK, False, tq)
    bias_p = _bias_tiles(rel_table, tq, tq, 0)
    a = _dsa_cols(pp["q"], pp["qi"], pp["wi"], pp["k"].reshape(nb, seq, ATT_KV),
                  pp["v"].reshape(nb, seq // tq, ATT_KV, tq), pp["ki"].reshape(nb, seq, IDX_DIM), bias_p,
                  nb=nb, seq=seq, tq=tq, n_sel=min(TOPK_MAX, seq // 4))
    y_p = dense_out(a.reshape(mp, ATT_Q), pp, hp, 512).reshape(nb, seq, D_MODEL)

    ms = ns * n_new
    hs, ps = dense_in(x_sample.reshape(ms, D_MODEL), 128, n_new, True, None)
    tk_s = V7X_LANES
    lp = -(-total // tk_s) * tk_s
    cat = lambda c, new, w: jnp.pad(
        jnp.concatenate([c[l].reshape(ns, past, w).astype(BF16), new.reshape(ns, n_new, w)], axis=1),
        ((0, 0), (0, lp - total), (0, 0)))
    bias_s = _bias_tiles(rel_table, n_new, tk_s, 1)
    a = _dsa_rows(ps["q"].reshape(ns, n_new, ATT_Q), ps["qi"].reshape(ns, n_new, IDX_Q),
                  ps["wi"].reshape(ns, n_new, V7X_LANES),
                  cat(cache_k, ps["k"], ATT_KV), cat(cache_v, ps["v"], ATT_KV), cat(cache_kidx, ps["ki"], IDX_DIM),
                  bias_s, tq=n_new, tk=tk_s, n_sel=min(TOPK_MAX, total // 4), q_start=past, n_keys=total)
    y_s = dense_out(a.reshape(ms, ATT_Q), ps, hs, 128).reshape(ns, n_new, D_MODEL)

    kv5 = lambda x, b, t: x.reshape(1, b, t, KV_HEADS, HEAD_DIM)
    return (y_p, y_s,
            kv5(pp["k_f32"], nb, seq), kv5(pp["v_f32"], nb, seq), pp["ki_f32"].reshape(1, nb, seq, IDX_DIM),
            kv5(ps["k_f32"], ns, n_new), kv5(ps["v_f32"], ns, n_new), ps["ki_f32"].reshape(1, ns, n_new, IDX_DIM),
            ps["vg"].reshape(1, ns, n_new, D_MODEL))
```

```python
import functools
import math

import numpy as np
import jax
import jax.numpy as jnp
from jax import lax
from jax.experimental import pallas as pl
from jax.experimental.pallas import tpu as pltpu

D_MODEL = 1024
CHUNK = 64
N_HEADS = 16
HEAD_DIM = 64
KV_HEADS = 4
GROUP = N_HEADS // KV_HEADS
IDX_HEADS = 8
IDX_DIM = 64
TOPK_MAX = 256
GM_CHUNK = 128
GM_GROUPS = 4
GM_GROUP_DIM = D_MODEL // GM_GROUPS
D_FF = 2816
REL_BUCKETS = 32
REL_MAX_DIST = 128
LN_EPS = 1e-5
ATT_Q = N_HEADS * HEAD_DIM
ATT_KV = KV_HEADS * HEAD_DIM
IDX_Q = IDX_HEADS * IDX_DIM

V7X_LANES = 128
V7X_SUBLANES = 8
BF16_ROWS = 2 * V7X_SUBLANES
FAR_LOOKAHEAD = 1
NEAR_LOOKAHEAD = 4
V7X_MXU_DIM = 256
V7X_VMEM_BYTES = 64 * 1024 * 1024

FF_BLOCK = V7X_MXU_DIM
LOG2_E = math.log2(math.e)
Q_SCALE = LOG2_E * HEAD_DIM ** -0.5
NEG_LOGIT = -1e30
INT_MIN = -2 ** 31
F32 = jnp.float32
BF16 = jnp.bfloat16


def _vmem_limit(nbytes):
    return int(min(max(nbytes, 16 * 1024 * 1024), V7X_VMEM_BYTES - 6 * 1024 * 1024))


def _resident(shape):
    zeros = (0,) * len(shape)
    return pl.BlockSpec(shape, lambda *_: zeros, pipeline_mode=pl.Buffered(1))


def _layer_norm(x, g, b):
    mu = jnp.mean(x, axis=-1, keepdims=True)
    xc = x - mu
    var = jnp.mean(xc * xc, axis=-1, keepdims=True)
    return xc * lax.rsqrt(var + LN_EPS) * g + b


def _swiglu(xb, wg_ref, wu_ref, wd_ref, act_scr):
    for c in range(D_FF // FF_BLOCK):
        cols = slice(c * FF_BLOCK, (c + 1) * FF_BLOCK)
        g = jnp.dot(xb, wg_ref[:, cols], preferred_element_type=F32)
        u = jnp.dot(xb, wu_ref[:, cols], preferred_element_type=F32)
        act_scr[:, cols] = (g * jax.nn.sigmoid(g) * u).astype(BF16)
    return jnp.dot(act_scr[...], wd_ref[...], preferred_element_type=F32)


def _sortable_key(score):
    score = jnp.where(score == 0.0, 0.0, score)
    bits = pltpu.bitcast(score, jnp.int32)
    return bits ^ ((bits >> 31) & 0x7FFFFFFF)


def _kth_largest_key(count_ge, f_sel, *, bits, lowest, cnt_lowest):
    def bit_body(i, carry):
        prefix, cnt_t = carry
        cand = prefix + lax.shift_left(jnp.int32(1), jnp.int32(bits - 1) - i)
        cnt = count_ge(cand)
        take = cnt >= f_sel
        return jnp.where(take, cand, prefix), jnp.where(take, cnt, cnt_t)

    return lax.fori_loop(0, bits, bit_body, (jnp.full(cnt_lowest.shape, lowest, jnp.int32), cnt_lowest))


def _ffn_ln_kernel(x_ref, wg_ref, wu_ref, wd_ref, g_ref, b_ref, o_ref, act_scr, *, alpha):
    x = x_ref[...]
    y = _swiglu(x.astype(BF16), wg_ref, wu_ref, wd_ref, act_scr)
    o_ref[...] = _layer_norm(alpha * x + 0.5 * y, g_ref[...], b_ref[...])


def _ffn_ln(x, wg, wu, wd, g, b, *, tm, alpha):
    m = x.shape[0]
    row = lambda i: (i, 0)
    weights = 3 * D_MODEL * D_FF * 2
    tiles = tm * D_MODEL * 4 * 4 + tm * D_FF * (2 + 3 * 4)
    return pl.pallas_call(
        functools.partial(_ffn_ln_kernel, alpha=alpha),
        out_shape=jax.ShapeDtypeStruct((m, D_MODEL), F32),
        grid=(m // tm,),
        in_specs=[pl.BlockSpec((tm, D_MODEL), row),
                  _resident((D_MODEL, D_FF)), _resident((D_MODEL, D_FF)), _resident((D_FF, D_MODEL)),
                  _resident((1, D_MODEL)), _resident((1, D_MODEL))],
        out_specs=pl.BlockSpec((tm, D_MODEL), row),
        scratch_shapes=[pltpu.VMEM((tm, D_FF), BF16)],
        compiler_params=pltpu.CompilerParams(dimension_semantics=("arbitrary",),
                                             vmem_limit_bytes=_vmem_limit(weights + tiles)),
        name="ffn_ln",
    )(x, wg, wu, wd, g, b)


def _mix_in_kernel(h_ref, wqkv_ref, widx_ref, wzg_ref, wga_ref, wgb_ref, gmg_ref, gmb_ref, ws_ref, bst_ref,
                   *out_refs, names, tm, gate_rows, kv_block):
    o = dict(zip(names, out_refs))
    hb = h_ref[...].astype(BF16)

    zg = jnp.dot(hb, wzg_ref[...], preferred_element_type=F32)

    qkv = jnp.dot(hb, wqkv_ref[...], preferred_element_type=F32)
    q = qkv[:, :ATT_Q] * Q_SCALE
    k = qkv[:, ATT_Q:ATT_Q + ATT_KV]
    v = qkv[:, ATT_Q + ATT_KV:]
    o["k_f32"][...] = k
    o["v_f32"][...] = v
    o["k"][...] = k.astype(BF16)

    idx = jnp.dot(hb, widx_ref[...], preferred_element_type=F32)
    qi = idx[:, :IDX_Q]
    ki = idx[:, IDX_Q:IDX_Q + IDX_DIM]
    wi = idx[:, IDX_Q + V7X_LANES:]
    o["ki_f32"][...] = ki
    o["ki"][...] = ki.astype(BF16)

    if kv_block is None:
        o["q"][...] = q.astype(BF16)
        o["v"][...] = v.astype(BF16)
        o["qi"][...] = qi.astype(BF16)
        o["wi"][...] = wi
    else:
        o["q"][...] = q.T.astype(BF16)
        o["qi"][...] = qi.T.astype(BF16)
        o["wi"][...] = wi.T[:IDX_HEADS]
        vt = v.T.astype(BF16)
        for c in range(tm // kv_block):
            o["v"][c] = vt[:, c * kv_block:(c + 1) * kv_block]

    o["ga"][...] = jnp.dot(hb, wga_ref[...], preferred_element_type=F32)
    o["gb"][...] = jnp.dot(hb, wgb_ref[...], preferred_element_type=F32)

    zg = jax.nn.gelu(zg)
    u = zg[:, :D_MODEL]
    vg = _layer_norm(zg[:, D_MODEL:], gmg_ref[...], gmb_ref[...])
    if "vg" in o:
        o["vg"][...] = vg
    vgb = vg.astype(BF16)

    r = lax.broadcasted_iota(jnp.int32, (GM_CHUNK, GM_CHUNK), 0)
    c = lax.broadcasted_iota(jnp.int32, (GM_CHUNK, GM_CHUNK), 1)
    keep = (r >= c) & ((r // gate_rows) == (c // gate_rows))
    for grp in range(GM_GROUPS):
        cols = slice(grp * GM_GROUP_DIM, (grp + 1) * GM_GROUP_DIM)
        w = jnp.where(keep, ws_ref[grp], 0.0).astype(BF16)
        bias = bst_ref[:, grp:grp + 1]
        for ch in range(tm // GM_CHUNK):
            rows = slice(ch * GM_CHUNK, (ch + 1) * GM_CHUNK)
            s = jnp.dot(w, vgb[rows, cols], preferred_element_type=F32) + bias
            o["gate"][rows, cols] = (u[rows, cols] * s).astype(BF16)


def _mix_in(h, wqkv, widx, wzg, wga, wgb, gmg, gmb, ws, bst, *, tm, gate_rows, emit_vg, kv_block):
    m = h.shape[0]
    row = lambda i: (i, 0)
    col = lambda i: (0, i)
    rows = lambda n, dt: (jax.ShapeDtypeStruct((m, n), dt), pl.BlockSpec((tm, n), row))
    cols = lambda n, dt: (jax.ShapeDtypeStruct((n, m), dt), pl.BlockSpec((n, tm), col))
    outs = {"k_f32": rows(ATT_KV, F32), "v_f32": rows(ATT_KV, F32), "ki_f32": rows(IDX_DIM, F32),
            "k": rows(ATT_KV, BF16), "ki": rows(IDX_DIM, BF16),
            "gate": rows(D_MODEL, BF16), "ga": rows(D_MODEL, F32), "gb": rows(D_MODEL, F32)}
    if kv_block is None:
        outs.update(q=rows(ATT_Q, BF16), v=rows(ATT_KV, BF16), qi=rows(IDX_Q, BF16), wi=rows(V7X_LANES, F32))
    else:
        assert tm % kv_block == 0
        outs.update(q=cols(ATT_Q, BF16), qi=cols(IDX_Q, BF16), wi=cols(IDX_HEADS, F32),
                    v=(jax.ShapeDtypeStruct((m // kv_block, ATT_KV, kv_block), BF16),
                       pl.BlockSpec((tm // kv_block, ATT_KV, kv_block), lambda i: (i, 0, 0))))
    if emit_vg:
        outs["vg"] = rows(D_MODEL, F32)
    names = tuple(outs)
    n_in = wqkv.shape[1] + widx.shape[1] + wzg.shape[1] + 2 * D_MODEL
    weights = D_MODEL * n_in * 2
    tiles = tm * (n_in * 4 * 3 + D_MODEL * 4 * 2)
    res = pl.pallas_call(
        functools.partial(_mix_in_kernel, names=names, tm=tm, gate_rows=gate_rows, kv_block=kv_block),
        out_shape=[outs[n][0] for n in names],
        grid=(m // tm,),
        in_specs=[pl.BlockSpec((tm, D_MODEL), row),
                  _resident(wqkv.shape), _resident(widx.shape), _resident(wzg.shape),
                  _resident(wga.shape), _resident(wgb.shape),
                  _resident((1, D_MODEL)), _resident((1, D_MODEL)),
                  _resident(ws.shape), _resident(bst.shape)],
        out_specs=[outs[n][1] for n in names],
        compiler_params=pltpu.CompilerParams(dimension_semantics=("arbitrary",),
                                             vmem_limit_bytes=_vmem_limit(weights + tiles)),
        name="mix_in",
    )(h, wqkv, widx, wzg, wga, wgb, gmg, gmb, ws, bst)
    return dict(zip(names, res))


def _bucket_thresholds():
    half = REL_BUCKETS // 2
    max_exact = half // 2
    n = np.arange(1, 2 * REL_MAX_DIST, dtype=np.float32)
    large = max_exact + (np.log(n / np.float32(max_exact)) / np.float32(math.log(REL_MAX_DIST / max_exact))
                         * np.float32(half - max_exact)).astype(np.int32)
    large = np.minimum(large, half - 1)
    thr = [int(np.argmax(large >= b)) + 1 for b in range(max_exact + 1, half)]
    return max_exact, half, thr


def _bias_kernel(table_ref, o_ref, *, shape, key_axis, tk):
    max_exact, half, thr = _bucket_thresholds()
    d = pl.program_id(0)
    h = pl.program_id(1)
    rel = (lax.broadcasted_iota(jnp.int32, shape, key_axis) - lax.broadcasted_iota(jnp.int32, shape, 1 - key_axis)
           + (d - 1) * tk)
    n = jnp.abs(rel)
    large = jnp.full(shape, max_exact, jnp.int32)
    for t in thr:
        large = large + jnp.where(n >= t, 1, 0)
    bucket = jnp.where(rel > 0, half, 0) + jnp.where(n < max_exact, n, large)
    acc = jnp.zeros(shape, F32)
    for b in range(REL_BUCKETS):
        acc = acc + jnp.where(bucket == b, table_ref[b, h], 0.0)
    o_ref[...] = (acc - table_ref[half - 1, h]) * LOG2_E


def _bias_tiles(rel_table, tq, tk, key_axis):
    shape = (tq, tk) if key_axis == 1 else (tk, tq)
    return pl.pallas_call(
        functools.partial(_bias_kernel, shape=shape, key_axis=key_axis, tk=tk),
        out_shape=jax.ShapeDtypeStruct((2, N_HEADS) + shape, F32),
        grid=(2, N_HEADS),
        in_specs=[pl.BlockSpec(memory_space=pltpu.SMEM)],
        out_specs=pl.BlockSpec((None, None) + shape, lambda d, h: (d, h, 0, 0)),
        compiler_params=pltpu.CompilerParams(dimension_semantics=("arbitrary", "arbitrary")),
        name="rel_bias",
    )(rel_table)


def _dsa_cols_kernel(qt_ref, qit_ref, wit_ref, k_ref, vt_ref, ki_ref, bias_ref, o_ref,
                     hi_scr, d1_scr, d0_scr, m_scr, acc_scr, mb_scr, sa_scr, sb_scr, *, tq, tk, n_sel):
    f_sel = float(n_sel)
    i16 = jnp.int16
    j = pl.program_id(1)
    nkb = j + 1
    pos = j * tq + lax.broadcasted_iota(jnp.int32, (1, tq), 1)
    n_vis = (pos // CHUNK + 1) * CHUNK
    need_sel = n_vis > n_sel

    n_pairs = (nkb + 1) // 2

    def score_body(t, carry):
        for kb in (2 * t, 2 * t + 1):
            off = pl.multiple_of(kb * tk, tk)
            kit = ki_ref[pl.ds(off, tk), :]
            s = jnp.zeros((tk, tq), F32)
            for h in range(IDX_HEADS):
                d = jnp.dot(kit, qit_ref[h * IDX_DIM:(h + 1) * IDX_DIM, :], preferred_element_type=F32)
                s = s + wit_ref[h:h + 1, :] * jnp.maximum(d, 0.0)
            kpos = off + lax.broadcasted_iota(jnp.int32, (tk, tq), 0)
            key = jnp.where(kpos < n_vis, _sortable_key(s), INT_MIN)
            hi_scr[kb] = (key >> 16).astype(i16)
            d1_scr[kb] = ((key >> 8) & 0xFF).astype(i16)
            d0_scr[kb] = (key & 0xFF).astype(i16)
        return carry

    lax.fori_loop(0, n_pairs, score_body, 0)

    one_bf, zero_bf = jnp.ones((), BF16), jnp.zeros((), BF16)

    def counter(digit_scr):
        def count_ge(cand):
            cand16 = cand.astype(i16)

            def body(t, part):
                tiles = []
                for kb in (2 * t, 2 * t + 1):
                    hit = jnp.where(digit_scr[kb] >= cand16, one_bf, zero_bf)
                    tiles += [hit[r * BF16_ROWS:(r + 1) * BF16_ROWS] for r in range(tk // BF16_ROWS)]
                while len(tiles) > 1:
                    tiles = [a + b for a, b in zip(tiles[::2], tiles[1::2])]
                return part + tiles[0].astype(F32)

            part = lax.fori_loop(0, n_pairs, body, jnp.zeros((BF16_ROWS, tq), F32))
            return jnp.sum(part, axis=0, keepdims=True)

        return count_ge

    def refine(dst_scr, upper_scr, upper_thr):
        thr16 = upper_thr.astype(i16)

        def body(t, carry):
            for kb in (2 * t, 2 * t + 1):
                up = upper_scr[kb]
                dst_scr[kb] = jnp.where(up > thr16, i16(256), jnp.where(up == thr16, dst_scr[kb], i16(-1)))
            return carry

        lax.fori_loop(0, n_pairs, body, 0)

    cnt_all = jnp.zeros((1, tq), F32)
    thr_hi, cnt = _kth_largest_key(counter(hi_scr), f_sel, bits=16, lowest=-2 ** 15, cnt_lowest=cnt_all)
    thr_hi = jnp.where(need_sel, thr_hi, -2 ** 15)
    refine(d1_scr, hi_scr, thr_hi)
    thr_d1, cnt = _kth_largest_key(counter(d1_scr), f_sel, bits=8, lowest=0, cnt_lowest=cnt)
    thr_d1 = jnp.where(need_sel, thr_d1, 0)
    refine(d0_scr, d1_scr, thr_d1)
    count_d0 = counter(d0_scr)
    thr_d0, cnt = _kth_largest_key(count_d0, f_sel, bits=8, lowest=0, cnt_lowest=cnt)
    thr_d0 = jnp.where(need_sel, thr_d0, 1)

    tie = need_sel & (cnt != f_sel)
    any_tie = jnp.max(jnp.where(tie, 1.0, 0.0)) > 0.0

    @pl.when(any_tie)
    def _():
        room = f_sel - count_d0(thr_d0 + 1)
        rr = lax.broadcasted_iota(jnp.int32, (tk, tk), 0)
        cc = lax.broadcasted_iota(jnp.int32, (tk, tk), 1)
        lower = jnp.where(rr >= cc, 1.0, 0.0).astype(BF16)
        thr16 = thr_d0.astype(i16)
        demoted = (thr_d0 - 1).astype(i16)

        def body(kb, seen):
            t = d0_scr[kb]
            eq = t == thr16
            rank = jnp.dot(lower, jnp.where(eq, one_bf, zero_bf), preferred_element_type=F32) + seen
            drop = jnp.where(tie & (rank > room), 1.0, 0.0).astype(BF16) > zero_bf
            d0_scr[kb] = jnp.where(eq & drop, demoted, t)
            return rank[tk - 1:tk, :]

        lax.fori_loop(0, nkb, body, jnp.zeros((1, tq), F32))

    sel16 = thr_d0.astype(i16)

    m_scr[...] = jnp.full(m_scr.shape, NEG_LOGIT, F32)
    acc_scr[...] = jnp.zeros(acc_scr.shape, F32)
    ones_rows = jnp.ones((acc_scr.shape[1] - HEAD_DIM, tk), BF16)

    def set_mask_tile(kb):
        mb_scr[...] = jnp.where(d0_scr[kb] >= sel16, zero_bf, jnp.asarray(NEG_LOGIT, BF16)).astype(F32)

    def produce(kb, h, dst_scr, near=None):
        off = pl.multiple_of(kb * tk, tk)
        g = h // GROUP
        kg = k_ref[pl.ds(off, tk), g * HEAD_DIM:(g + 1) * HEAD_DIM]
        s = jnp.dot(kg, qt_ref[h * HEAD_DIM:(h + 1) * HEAD_DIM, :], preferred_element_type=F32) + mb_scr[...]
        dst_scr[h] = s if near is None else s + bias_ref[near, h]

    def consume(kb, h, src_scr):
        g = h // GROUP
        ps, alphas = [], []
        for c in range(tq // V7X_LANES):
            lanes = slice(c * V7X_LANES, (c + 1) * V7X_LANES)
            m_old = m_scr[h, :, lanes]
            m_new = jnp.maximum(m_old, jnp.max(src_scr[h, :, lanes], axis=0, keepdims=True))
            alphas.append(jnp.exp2(m_old - m_new))
            ps.append(jnp.exp2(src_scr[h, :, lanes] - m_new).astype(BF16))
            m_scr[h, :, lanes] = m_new
        vg = jnp.concatenate([vt_ref[kb, g * HEAD_DIM:(g + 1) * HEAD_DIM, :], ones_rows], axis=0)
        pv = jnp.dot(vg, jnp.concatenate(ps, axis=1), preferred_element_type=F32)
        acc_scr[h] = jnp.concatenate(alphas, axis=1) * acc_scr[h] + pv

    n_far = jnp.maximum(nkb - 2, 0)

    def far_step(kb, cur_scr, nxt_scr):
        if nxt_scr is None:
            for h in range(N_HEADS):
                consume(kb, h, cur_scr)
            return
        kb_next = jnp.minimum(kb + 1, n_far - 1)

        @pl.when(kb_next >= 0)
        def _():
            set_mask_tile(kb_next)

        for h in range(-FAR_LOOKAHEAD, N_HEADS):
            if h + FAR_LOOKAHEAD < N_HEADS:
                produce(kb_next, h + FAR_LOOKAHEAD, nxt_scr)
            if h >= 0:
                consume(kb, h, cur_scr)

    @pl.when(n_far > 0)
    def _():
        set_mask_tile(0)
        for h in range(N_HEADS):
            produce(0, h, sa_scr)

    def pair_body(t, carry):
        far_step(2 * t, sa_scr, sb_scr)
        far_step(2 * t + 1, sb_scr, sa_scr)
        return carry

    lax.fori_loop(0, n_far // 2, pair_body, 0)

    @pl.when(n_far % 2 == 1)
    def _():
        far_step(n_far - 1, sa_scr, None)

    def near_body(kb, carry):
        near = kb - (nkb - 2)
        set_mask_tile(kb)
        for h in range(-NEAR_LOOKAHEAD, N_HEADS):
            if h + NEAR_LOOKAHEAD < N_HEADS:
                produce(kb, h + NEAR_LOOKAHEAD, sa_scr, near)
            if h >= 0:
                consume(kb, h, sa_scr)
        return carry

    lax.fori_loop(n_far, nkb, near_body, 0)

    def normalised(h):
        return acc_scr[h, :HEAD_DIM, :] / acc_scr[h, HEAD_DIM:HEAD_DIM + 1, :]

    for h in range(0, N_HEADS, 2):
        pair = jnp.concatenate([normalised(h), normalised(h + 1)], axis=0)
        o_ref[:, h * HEAD_DIM:(h + 2) * HEAD_DIM] = pair.T.astype(BF16)


def _dsa_cols(qt, qit, wit, k, vt, ki, bias, *, nb, seq, tq, n_sel):
    tk = tq
    nq = seq // tq
    assert tk >= REL_MAX_DIST and seq % tq == 0
    qmap = lambda b, j: (0, b * nq + j)
    once = lambda shape, imap: pl.BlockSpec(shape, imap, pipeline_mode=pl.Buffered(1))
    scratch = [pltpu.VMEM((nq, tk, tq), jnp.int16),
               pltpu.VMEM((nq, tk, tq), jnp.int16),
               pltpu.VMEM((nq, tk, tq), jnp.int16),
               pltpu.VMEM((N_HEADS, 1, tq), F32),
               pltpu.VMEM((N_HEADS, HEAD_DIM + BF16_ROWS, tq), F32),
               pltpu.VMEM((tk, tq), F32),
               pltpu.VMEM((N_HEADS, tk, tq), F32),
               pltpu.VMEM((N_HEADS, tk, tq), F32)]
    est = (seq * (2 * ATT_KV + V7X_LANES) * 2 + 3 * seq * tq * 2 + 4 * N_HEADS * tq * tk * 4
           + 2 * tq * (2 * ATT_Q + IDX_Q) * 2 + N_HEADS * HEAD_DIM * tq * 4
           + IDX_HEADS * tq * tk * 4 * 2 + 8 * tq * tk * 4)
    return pl.pallas_call(
        functools.partial(_dsa_cols_kernel, tq=tq, tk=tk, n_sel=n_sel),
        out_shape=jax.ShapeDtypeStruct((nb, seq, ATT_Q), BF16),
        grid=(nb, nq),
        in_specs=[pl.BlockSpec((ATT_Q, tq), qmap), pl.BlockSpec((IDX_Q, tq), qmap),
                  pl.BlockSpec((IDX_HEADS, tq), qmap),
                  once((None, seq, ATT_KV), lambda b, j: (b, 0, 0)),
                  once((None, nq, ATT_KV, tk), lambda b, j: (b, 0, 0, 0)),
                  once((None, seq, IDX_DIM), lambda b, j: (b, 0, 0)),
                  _resident(bias.shape)],
        out_specs=pl.BlockSpec((None, tq, ATT_Q), lambda b, j: (b, j, 0)),
        scratch_shapes=scratch,
        compiler_params=pltpu.CompilerParams(dimension_semantics=("arbitrary", "arbitrary"),
                                             vmem_limit_bytes=_vmem_limit(est)),
        name="dsa_cols",
    )(qt, qit, wit, k, vt, ki, bias)


def _dsa_rows_kernel(q_ref, qi_ref, wi_ref, k_ref, v_ref, ki_ref, bias_ref, o_ref,
                     keys_scr, qstk_scr, qistk_scr, wib_scr, *, tq, tk, n_sel, n_keys):
    f_sel = float(n_sel)
    nkb = -(-n_keys // tk)
    n_vis = jnp.full((tq, 1), n_keys, jnp.int32)

    for hh in range(N_HEADS):
        g, i = divmod(hh, GROUP)
        qstk_scr[g, i * tq:(i + 1) * tq, :] = q_ref[:, hh * HEAD_DIM:(hh + 1) * HEAD_DIM]
    for h in range(IDX_HEADS):
        qistk_scr[h * tq:(h + 1) * tq, :] = qi_ref[:, h * IDX_DIM:(h + 1) * IDX_DIM]
        wib_scr[h] = jnp.broadcast_to(wi_ref[:, h:h + 1], (tq, tk))

    nt_dims = (((1,), (1,)), ((), ()))

    def score_body(kb, carry):
        off = pl.multiple_of(kb * tk, tk)
        d = lax.dot_general(qistk_scr[...], ki_ref[pl.ds(off, tk), :], nt_dims, preferred_element_type=F32)
        s = jnp.zeros((tq, tk), F32)
        for h in range(IDX_HEADS):
            s = s + wib_scr[h] * jnp.maximum(d[h * tq:(h + 1) * tq], 0.0)
        kpos = off + lax.broadcasted_iota(jnp.int32, (tq, tk), 1)
        keys_scr[kb] = jnp.where(kpos < n_vis, _sortable_key(s), INT_MIN)
        return carry

    lax.fori_loop(0, nkb, score_body, 0)

    def count_ge(cand):
        candb = jnp.broadcast_to(cand, (tq, V7X_LANES))

        def body(kb, part):
            t = keys_scr[kb]
            for c in range(tk // V7X_LANES):
                part = part + jnp.where(t[:, c * V7X_LANES:(c + 1) * V7X_LANES] >= candb, 1.0, 0.0)
            return part

        part = lax.fori_loop(0, nkb, body, jnp.zeros((tq, V7X_LANES), F32))
        return jnp.sum(part, axis=1, keepdims=True)

    thr, cnt_thr = _kth_largest_key(count_ge, f_sel, bits=32, lowest=INT_MIN,
                                    cnt_lowest=jnp.zeros((tq, 1), F32))

    tie = (n_vis > n_sel) & (cnt_thr != f_sel)
    any_tie = jnp.max(jnp.where(tie, 1.0, 0.0)) > 0.0

    @pl.when(any_tie)
    def _():
        room = f_sel - count_ge(thr + 1)
        rr = lax.broadcasted_iota(jnp.int32, (tk, tk), 0)
        cc = lax.broadcasted_iota(jnp.int32, (tk, tk), 1)
        upper = jnp.where(rr <= cc, 1.0, 0.0).astype(BF16)

        def body(kb, seen):
            t = keys_scr[kb]
            eq = t == thr
            eqf = jnp.where(eq, 1.0, 0.0)
            rank = jnp.dot(eqf.astype(BF16), upper, preferred_element_type=F32) + seen
            keys_scr[kb] = jnp.where(eq & tie & (rank > room), thr - 1, t)
            return seen + jnp.sum(eqf, axis=1, keepdims=True)

        lax.fori_loop(0, nkb, body, jnp.zeros((tq, 1), F32))

    thr_sel = jnp.maximum(thr, INT_MIN + 1)

    lp = nkb * tk
    mask_add = jnp.concatenate([jnp.where(keys_scr[kb] >= thr_sel, 0.0, NEG_LOGIT) for kb in range(nkb)], axis=1)
    far = jnp.zeros((GROUP, tq, lp - 2 * tk), F32)
    for g in range(KV_HEADS):
        cols = slice(g * HEAD_DIM, (g + 1) * HEAD_DIM)
        heads = slice(g * GROUP, (g + 1) * GROUP)
        s = lax.dot_general(qstk_scr[g], k_ref[:, cols], nt_dims, preferred_element_type=F32)
        s = s.reshape(GROUP, tq, lp) + mask_add[None]
        s = s + jnp.concatenate([far, bias_ref[0, heads], bias_ref[1, heads]], axis=-1)
        p = jnp.exp2(s - jnp.max(s, axis=-1, keepdims=True))
        l = jnp.sum(p, axis=-1, keepdims=True)
        pv = jnp.dot(p.astype(BF16).reshape(GROUP * tq, lp), v_ref[:, cols], preferred_element_type=F32)
        o = pv.reshape(GROUP, tq, HEAD_DIM) / l
        for i in range(GROUP):
            hh = g * GROUP + i
            o_ref[:, hh * HEAD_DIM:(hh + 1) * HEAD_DIM] = o[i].astype(BF16)


def _dsa_rows(q, qi, wi, k, v, ki, bias, *, tq, tk, n_sel, q_start, n_keys):
    nb = q.shape[0]
    lp = k.shape[1]
    nkb = lp // tk
    qmap = lambda b: (b, 0, 0)
    assert tk >= REL_MAX_DIST and lp % tk == 0 and q_start == (nkb - 1) * tk and nkb == -(-n_keys // tk)
    scratch = [pltpu.VMEM((nkb, tq, tk), jnp.int32),
               pltpu.VMEM((KV_HEADS, GROUP * tq, HEAD_DIM), BF16),
               pltpu.VMEM((IDX_HEADS * tq, IDX_DIM), BF16),
               pltpu.VMEM((IDX_HEADS, tq, tk), F32)]
    est = (2 * lp * (2 * ATT_KV + V7X_LANES) * 2 + lp * tq * 4 + 2 * N_HEADS * tq * tk * 4
           + IDX_HEADS * tq * tk * 4 * 3 + 8 * N_HEADS * tq * lp * 4)
    return pl.pallas_call(
        functools.partial(_dsa_rows_kernel, tq=tq, tk=tk, n_sel=n_sel, n_keys=n_keys),
        out_shape=jax.ShapeDtypeStruct((nb, tq, ATT_Q), BF16),
        grid=(nb,),
        in_specs=[pl.BlockSpec((None, tq, ATT_Q), qmap), pl.BlockSpec((None, tq, IDX_Q), qmap),
                  pl.BlockSpec((None, tq, V7X_LANES), qmap),
                  pl.BlockSpec((None, lp, ATT_KV), qmap), pl.BlockSpec((None, lp, ATT_KV), qmap),
                  pl.BlockSpec((None, lp, IDX_DIM), qmap),
                  _resident(bias.shape)],
        out_specs=pl.BlockSpec((None, tq, ATT_Q), qmap),
        scratch_shapes=scratch,
        compiler_params=pltpu.CompilerParams(dimension_semantics=("arbitrary",),
                                             vmem_limit_bytes=_vmem_limit(est)),
        name="dsa_rows",
    )(q, qi, wi, k, v, ki, bias)


def _merge_ffn_kernel(a_ref, gate_ref, ga_ref, gb_ref, h_ref, wa_ref, wb_ref, wo_ref, g2_ref, b2_ref,
                      wg_ref, wu_ref, wd_ref, g3_ref, b3_ref, o_ref, act_scr, *, alpha):
    br_a = jnp.dot(a_ref[...], wa_ref[...], preferred_element_type=F32)
    br_b = jnp.dot(gate_ref[...], wb_ref[...], preferred_element_type=F32)
    merge = jax.nn.sigmoid(ga_ref[...]) * br_a + jax.nn.sigmoid(gb_ref[...]) * br_b
    mixed = jnp.dot(merge.astype(BF16), wo_ref[...], preferred_element_type=F32)
    h2 = _layer_norm(alpha * h_ref[...] + mixed, g2_ref[...], b2_ref[...])
    y = _swiglu(h2.astype(BF16), wg_ref, wu_ref, wd_ref, act_scr)
    o_ref[...] = _layer_norm(alpha * h2 + 0.5 * y, g3_ref[...], b3_ref[...])


def _merge_ffn(a, gate, ga, gb, h, wa, wb, wo, g2, b2, wg, wu, wd, g3, b3, *, tm, alpha):
    m = h.shape[0]
    row = lambda i: (i, 0)
    tile = lambda: pl.BlockSpec((tm, D_MODEL), row)
    vec = lambda: _resident((1, D_MODEL))
    sq = lambda: _resident((D_MODEL, D_MODEL))
    weights = (3 * D_MODEL * D_MODEL + 3 * D_MODEL * D_FF) * 2
    tiles = tm * D_MODEL * 4 * 12 + tm * D_FF * (2 + 3 * 4)
    return pl.pallas_call(
        functools.partial(_merge_ffn_kernel, alpha=alpha),
        out_shape=jax.ShapeDtypeStruct((m, D_MODEL), F32),
        grid=(m // tm,),
        in_specs=[tile(), tile(), tile(), tile(), tile(), sq(), sq(), sq(), vec(), vec(),
                  _resident((D_MODEL, D_FF)), _resident((D_MODEL, D_FF)), _resident((D_FF, D_MODEL)),
                  vec(), vec()],
        out_specs=tile(),
        scratch_shapes=[pltpu.VMEM((tm, D_FF), BF16)],
        compiler_params=pltpu.CompilerParams(dimension_semantics=("arbitrary",),
                                             vmem_limit_bytes=_vmem_limit(weights + tiles)),
        name="merge_ffn",
    )(a, gate, ga, gb, h, wa, wb, wo, g2, b2, wg, wu, wd, g3, b3)


def _split_w_in(w_in):
    o_k = ATT_Q
    o_qi = ATT_Q + 2 * ATT_KV
    o_ki = o_qi + IDX_Q
    o_wi = o_ki + IDX_DIM
    o_zg = o_wi + IDX_HEADS
    o_ga = o_zg + 2 * D_MODEL
    o_gb = o_ga + D_MODEL
    wb = w_in.astype(BF16)
    pad = lambda w, n: jnp.pad(w, ((0, 0), (0, n - w.shape[1])))
    widx = jnp.concatenate([wb[:, o_qi:o_ki], pad(wb[:, o_ki:o_wi], V7X_LANES),
                            pad(wb[:, o_wi:o_zg], V7X_LANES)], axis=1)
    return wb[:, :o_qi], widx, wb[:, o_zg:o_ga], wb[:, o_ga:o_gb], wb[:, o_gb:]


def kernel(x_prompt, x_sample, cache_k, cache_v, cache_kidx, rel_table, ln1_g, ln1_b, ffn1_wg, ffn1_wu, ffn1_wd, w_in, gm_ln_g, gm_ln_b, gm_ws, gm_bs, w_br_a, w_br_b, w_out, ln2_g, ln2_b, ffn2_wg, ffn2_wu, ffn2_wd, ln3_g, ln3_b):
    depth = ln1_g.shape[0]
    assert depth == 1, "single-layer step"
    alpha = (2 * depth) ** 0.25
    nb, seq, _ = x_prompt.shape
    ns, n_new, _ = x_sample.shape
    past = cache_k.shape[2]
    total = past + n_new
    l = 0
    vec = lambda p: p[l].reshape(1, D_MODEL)
    bf = lambda p: p[l].astype(BF16)

    w_qkv, w_idx, w_zg, w_ga, w_gb = _split_w_in(w_in[l])
    f1 = (bf(ffn1_wg), bf(ffn1_wu), bf(ffn1_wd))
    f2 = (bf(ffn2_wg), bf(ffn2_wu), bf(ffn2_wd))
    br = (bf(w_br_a), bf(w_br_b), bf(w_out))

    def dense_in(x, tm, gate_rows, emit_vg, kv_block):
        h = _ffn_ln(x, *f1, vec(ln1_g), vec(ln1_b), tm=tm, alpha=alpha)
        reps = GM_CHUNK // gate_rows
        ws = jnp.tile(gm_ws[l][:, :gate_rows, :gate_rows], (1, reps, reps))
        bst = jnp.tile(gm_bs[l][:, :gate_rows].T, (reps, 1))
        parts = _mix_in(h, w_qkv, w_idx, w_zg, w_ga, w_gb, vec(gm_ln_g), vec(gm_ln_b), ws, bst,
                        tm=tm, gate_rows=gate_rows, emit_vg=emit_vg, kv_block=kv_block)
        return h, parts

    def dense_out(a, p, h, tm):
        return _merge_ffn(a, p["gate"], p["ga"], p["gb"], h, *br, vec(ln2_g), vec(ln2_b), *f2,
                          vec(ln3_g), vec(ln3_b), tm=tm, alpha=alpha)

    mp = nb * seq
    tq = 256
    hp, pp = dense_in(x_prompt.reshape(mp, D_MODEL), 512, GM_CHUNK, False, tq)
    bias_p = _bias_tiles(rel_table, tq, tq, 0)
    a = _dsa_cols(pp["q"], pp["qi"], pp["wi"], pp["k"].reshape(nb, seq, ATT_KV),
                  pp["v"].reshape(nb, seq // tq, ATT_KV, tq), pp["ki"].reshape(nb, seq, IDX_DIM), bias_p,
                  nb=nb, seq=seq, tq=tq, n_sel=min(TOPK_MAX, seq // 4))
    y_p = dense_out(a.reshape(mp, ATT_Q), pp, hp, 512).reshape(nb, seq, D_MODEL)

    ms = ns * n_new
    hs, ps = dense_in(x_sample.reshape(ms, D_MODEL), 128, n_new, True, None)
    tk_s = V7X_LANES
    lp = -(-total // tk_s) * tk_s
    cat = lambda c, new, w: jnp.pad(
        jnp.concatenate([c[l].reshape(ns, past, w).astype(BF16), new.reshape(ns, n_new, w)], axis=1),
        ((0, 0), (0, lp - total), (0, 0)))
    bias_s = _bias_tiles(rel_table, n_new, tk_s, 1)
    a = _dsa_rows(ps["q"].reshape(ns, n_new, ATT_Q), ps["qi"].reshape(ns, n_new, IDX_Q),
                  ps["wi"].reshape(ns, n_new, V7X_LANES),
                  cat(cache_k, ps["k"], ATT_KV), cat(cache_v, ps["v"], ATT_KV), cat(cache_kidx, ps["ki"], IDX_DIM),
                  bias_s, tq=n_new, tk=tk_s, n_sel=min(TOPK_MAX, total // 4), q_start=past, n_keys=total)
    y_s = dense_out(a.reshape(ms, ATT_Q), ps, hs, 128).reshape(ns, n_new, D_MODEL)

    kv5 = lambda x, b, t: x.reshape(1, b, t, KV_HEADS, HEAD_DIM)
    return (y_p, y_s,
            kv5(pp["k_f32"], nb, seq), kv5(pp["v_f32"], nb, seq), pp["ki_f32"].reshape(1, nb, seq, IDX_DIM),
            kv5(ps["k_f32"], ns, n_new), kv5(ps["v_f32"], ns, n_new), ps["ki_f32"].reshape(1, ns, n_new, IDX_DIM),
            ps["vg"].reshape(1, ns, n_new, D_MODEL))
```

```python
import functools
import math

import numpy as np
import jax
import jax.numpy as jnp
from jax import lax
from jax.experimental import pallas as pl
from jax.experimental.pallas import tpu as pltpu

D_MODEL = 1024
CHUNK = 64
N_HEADS = 16
HEAD_DIM = 64
KV_HEADS = 4
GROUP = N_HEADS // KV_HEADS
IDX_HEADS = 8
IDX_DIM = 64
TOPK_MAX = 256
GM_CHUNK = 128
GM_GROUPS = 4
GM_GROUP_DIM = D_MODEL // GM_GROUPS
D_FF = 2816
REL_BUCKETS = 32
REL_MAX_DIST = 128
LN_EPS = 1e-5
ATT_Q = N_HEADS * HEAD_DIM
ATT_KV = KV_HEADS * HEAD_DIM
IDX_Q = IDX_HEADS * IDX_DIM

V7X_LANES = 128
V7X_SUBLANES = 8
BF16_ROWS = 2 * V7X_SUBLANES
FAR_LOOKAHEAD = 1
NEAR_LOOKAHEAD = 4
V7X_MXU_DIM = 256
V7X_VMEM_BYTES = 64 * 1024 * 1024

FF_BLOCK = V7X_MXU_DIM
LOG2_E = math.log2(math.e)
Q_SCALE = LOG2_E * HEAD_DIM ** -0.5
NEG_LOGIT = -1e30
INT_MIN = -2 ** 31
KEY_ABOVE_NEG_INF = -(0x7F800000)
F32 = jnp.float32
BF16 = jnp.bfloat16


def _vmem_limit(nbytes):
    return int(min(max(nbytes, 16 * 1024 * 1024), V7X_VMEM_BYTES - 6 * 1024 * 1024))


def _resident(shape):
    zeros = (0,) * len(shape)
    return pl.BlockSpec(shape, lambda *_: zeros, pipeline_mode=pl.Buffered(1))


def _layer_norm(x, g, b):
    mu = jnp.mean(x, axis=-1, keepdims=True)
    xc = x - mu
    var = jnp.mean(xc * xc, axis=-1, keepdims=True)
    return xc * lax.rsqrt(var + LN_EPS) * g + b


def _swiglu(xb, wg_ref, wu_ref, wd_ref, act_scr):
    for c in range(D_FF // FF_BLOCK):
        cols = slice(c * FF_BLOCK, (c + 1) * FF_BLOCK)
        g = jnp.dot(xb, wg_ref[:, cols], preferred_element_type=F32)
        u = jnp.dot(xb, wu_ref[:, cols], preferred_element_type=F32)
        act_scr[:, cols] = (g * jax.nn.sigmoid(g) * u).astype(BF16)
    return jnp.dot(act_scr[...], wd_ref[...], preferred_element_type=F32)


def _sortable_key(score):
    score = jnp.where(score == 0.0, 0.0, score)
    bits = pltpu.bitcast(score, jnp.int32)
    return bits ^ ((bits >> 31) & 0x7FFFFFFF)


def _kth_largest_key(count_ge, f_sel, *, bits, lowest, cnt_lowest):
    def bit_body(i, carry):
        prefix, cnt_t = carry
        cand = prefix + lax.shift_left(jnp.int32(1), jnp.int32(bits - 1) - i)
        cnt = count_ge(cand)
        take = cnt >= f_sel
        return jnp.where(take, cand, prefix), jnp.where(take, cnt, cnt_t)

    return lax.fori_loop(0, bits, bit_body, (jnp.full(cnt_lowest.shape, lowest, jnp.int32), cnt_lowest))


def _ffn_ln_kernel(x_ref, wg_ref, wu_ref, wd_ref, g_ref, b_ref, o_ref, act_scr, *, alpha):
    x = x_ref[...]
    y = _swiglu(x.astype(BF16), wg_ref, wu_ref, wd_ref, act_scr)
    o_ref[...] = _layer_norm(alpha * x + 0.5 * y, g_ref[...], b_ref[...])


def _ffn_ln(x, wg, wu, wd, g, b, *, tm, alpha):
    m = x.shape[0]
    assert m % tm == 0
    row = lambda i: (i, 0)
    weights = 3 * D_MODEL * D_FF * 2
    tiles = tm * D_MODEL * 4 * 4 + tm * D_FF * (2 + 3 * 4)
    return pl.pallas_call(
        functools.partial(_ffn_ln_kernel, alpha=alpha),
        out_shape=jax.ShapeDtypeStruct((m, D_MODEL), F32),
        grid=(m // tm,),
        in_specs=[pl.BlockSpec((tm, D_MODEL), row),
                  _resident((D_MODEL, D_FF)), _resident((D_MODEL, D_FF)), _resident((D_FF, D_MODEL)),
                  _resident((1, D_MODEL)), _resident((1, D_MODEL))],
        out_specs=pl.BlockSpec((tm, D_MODEL), row),
        scratch_shapes=[pltpu.VMEM((tm, D_FF), BF16)],
        compiler_params=pltpu.CompilerParams(dimension_semantics=("arbitrary",),
                                             vmem_limit_bytes=_vmem_limit(weights + tiles)),
        name="ffn_ln",
    )(x, wg, wu, wd, g, b)


def _mix_in_kernel(h_ref, wqkv_ref, widx_ref, wzg_ref, wga_ref, wgb_ref, gmg_ref, gmb_ref, ws_ref, bst_ref,
                   *out_refs, names, tm, gate_rows, kv_block):
    o = dict(zip(names, out_refs))
    hb = h_ref[...].astype(BF16)

    zg = jnp.dot(hb, wzg_ref[...], preferred_element_type=F32)

    qkv = jnp.dot(hb, wqkv_ref[...], preferred_element_type=F32)
    q = qkv[:, :ATT_Q] * Q_SCALE
    k = qkv[:, ATT_Q:ATT_Q + ATT_KV]
    v = qkv[:, ATT_Q + ATT_KV:]
    o["k_f32"][...] = k
    o["v_f32"][...] = v
    o["k"][...] = k.astype(BF16)

    idx = jnp.dot(hb, widx_ref[...], preferred_element_type=F32)
    qi = idx[:, :IDX_Q]
    ki = idx[:, IDX_Q:IDX_Q + IDX_DIM]
    wi = idx[:, IDX_Q + V7X_LANES:]
    o["ki_f32"][...] = ki
    o["ki"][...] = ki.astype(BF16)

    if kv_block is None:
        o["q"][...] = q.astype(BF16)
        o["v"][...] = v.astype(BF16)
        o["qi"][...] = qi.astype(BF16)
        o["wi"][...] = wi
    else:
        o["q"][...] = q.T.astype(BF16)
        o["qi"][...] = qi.T.astype(BF16)
        o["wi"][...] = wi.T[:IDX_HEADS]
        vt = v.T.astype(BF16)
        for c in range(tm // kv_block):
            o["v"][c] = vt[:, c * kv_block:(c + 1) * kv_block]

    o["ga"][...] = jnp.dot(hb, wga_ref[...], preferred_element_type=F32)
    o["gb"][...] = jnp.dot(hb, wgb_ref[...], preferred_element_type=F32)

    zg = jax.nn.gelu(zg)
    u = zg[:, :D_MODEL]
    vg = _layer_norm(zg[:, D_MODEL:], gmg_ref[...], gmb_ref[...])
    if "vg" in o:
        o["vg"][...] = vg
    vgb = vg.astype(BF16)

    r = lax.broadcasted_iota(jnp.int32, (GM_CHUNK, GM_CHUNK), 0)
    c = lax.broadcasted_iota(jnp.int32, (GM_CHUNK, GM_CHUNK), 1)
    keep = (r >= c) & ((r // gate_rows) == (c // gate_rows))
    for grp in range(GM_GROUPS):
        cols = slice(grp * GM_GROUP_DIM, (grp + 1) * GM_GROUP_DIM)
        w = jnp.where(keep, ws_ref[grp], 0.0).astype(BF16)
        bias = bst_ref[:, grp:grp + 1]
        for ch in range(tm // GM_CHUNK):
            rows = slice(ch * GM_CHUNK, (ch + 1) * GM_CHUNK)
            s = jnp.dot(w, vgb[rows, cols], preferred_element_type=F32) + bias
            o["gate"][rows, cols] = (u[rows, cols] * s).astype(BF16)


def _mix_in(h, wqkv, widx, wzg, wga, wgb, gmg, gmb, ws, bst, *, tm, gate_rows, emit_vg, kv_block):
    m = h.shape[0]
    assert m % tm == 0 and tm % GM_CHUNK == 0
    row = lambda i: (i, 0)
    col = lambda i: (0, i)
    rows = lambda n, dt: (jax.ShapeDtypeStruct((m, n), dt), pl.BlockSpec((tm, n), row))
    cols = lambda n, dt: (jax.ShapeDtypeStruct((n, m), dt), pl.BlockSpec((n, tm), col))
    outs = {"k_f32": rows(ATT_KV, F32), "v_f32": rows(ATT_KV, F32), "ki_f32": rows(IDX_DIM, F32),
            "k": rows(ATT_KV, BF16), "ki": rows(IDX_DIM, BF16),
            "gate": rows(D_MODEL, BF16), "ga": rows(D_MODEL, F32), "gb": rows(D_MODEL, F32)}
    if kv_block is None:
        outs.update(q=rows(ATT_Q, BF16), v=rows(ATT_KV, BF16), qi=rows(IDX_Q, BF16), wi=rows(V7X_LANES, F32))
    else:
        assert tm % kv_block == 0
        outs.update(q=cols(ATT_Q, BF16), qi=cols(IDX_Q, BF16), wi=cols(IDX_HEADS, F32),
                    v=(jax.ShapeDtypeStruct((m // kv_block, ATT_KV, kv_block), BF16),
                       pl.BlockSpec((tm // kv_block, ATT_KV, kv_block), lambda i: (i, 0, 0))))
    if emit_vg:
        outs["vg"] = rows(D_MODEL, F32)
    names = tuple(outs)
    n_in = wqkv.shape[1] + widx.shape[1] + wzg.shape[1] + 2 * D_MODEL
    weights = D_MODEL * n_in * 2
    tiles = tm * (n_in * 4 * 3 + D_MODEL * 4 * 2)
    res = pl.pallas_call(
        functools.partial(_mix_in_kernel, names=names, tm=tm, gate_rows=gate_rows, kv_block=kv_block),
        out_shape=[outs[n][0] for n in names],
        grid=(m // tm,),
        in_specs=[pl.BlockSpec((tm, D_MODEL), row),
                  _resident(wqkv.shape), _resident(widx.shape), _resident(wzg.shape),
                  _resident(wga.shape), _resident(wgb.shape),
                  _resident((1, D_MODEL)), _resident((1, D_MODEL)),
                  _resident(ws.shape), _resident(bst.shape)],
        out_specs=[outs[n][1] for n in names],
        compiler_params=pltpu.CompilerParams(dimension_semantics=("arbitrary",),
                                             vmem_limit_bytes=_vmem_limit(weights + tiles)),
        name="mix_in",
    )(h, wqkv, widx, wzg, wga, wgb, gmg, gmb, ws, bst)
    return dict(zip(names, res))


def _bucket_thresholds():
    half = REL_BUCKETS // 2
    max_exact = half // 2
    n = np.arange(1, 2 * REL_MAX_DIST, dtype=np.float32)
    large = max_exact + (np.log(n / np.float32(max_exact)) / np.float32(math.log(REL_MAX_DIST / max_exact))
                         * np.float32(half - max_exact)).astype(np.int32)
    large = np.minimum(large, half - 1)
    thr = [int(np.argmax(large >= b)) + 1 for b in range(max_exact + 1, half)]
    return max_exact, half, thr


def _bias_kernel(table_ref, o_ref, *, shape, key_axis, tk):
    max_exact, half, thr = _bucket_thresholds()
    d = pl.program_id(0)
    h = pl.program_id(1)
    rel = (lax.broadcasted_iota(jnp.int32, shape, key_axis) - lax.broadcasted_iota(jnp.int32, shape, 1 - key_axis)
           + (d - 1) * tk)
    n = jnp.abs(rel)
    large = jnp.full(shape, max_exact, jnp.int32)
    for t in thr:
        large = large + jnp.where(n >= t, 1, 0)
    bucket = jnp.where(rel > 0, half, 0) + jnp.where(n < max_exact, n, large)
    acc = jnp.zeros(shape, F32)
    for b in range(REL_BUCKETS):
        acc = acc + jnp.where(bucket == b, table_ref[b, h], 0.0)
    o_ref[...] = (acc - table_ref[half - 1, h]) * LOG2_E


def _bias_tiles(rel_table, tq, tk, key_axis):
    shape = (tq, tk) if key_axis == 1 else (tk, tq)
    return pl.pallas_call(
        functools.partial(_bias_kernel, shape=shape, key_axis=key_axis, tk=tk),
        out_shape=jax.ShapeDtypeStruct((2, N_HEADS) + shape, F32),
        grid=(2, N_HEADS),
        in_specs=[pl.BlockSpec(memory_space=pltpu.SMEM)],
        out_specs=pl.BlockSpec((None, None) + shape, lambda d, h: (d, h, 0, 0)),
        compiler_params=pltpu.CompilerParams(dimension_semantics=("arbitrary", "arbitrary")),
        name="rel_bias",
    )(rel_table)


def _dsa_cols_kernel(qt_ref, qit_ref, wit_ref, k_ref, vt_ref, ki_ref, bias_ref, o_ref,
                     hi_scr, d1_scr, d0_scr, vis_scr, m_scr, acc_scr, mb_scr, sa_scr, sb_scr, *, tq, tk, n_sel):
    f_sel = float(n_sel)
    i16 = jnp.int16
    j = pl.program_id(1)
    nkb = j + 1
    pos = j * tq + lax.broadcasted_iota(jnp.int32, (1, tq), 1)
    n_vis = (pos // CHUNK + 1) * CHUNK
    need_sel = n_vis > n_sel

    @pl.when((pl.program_id(0) == 0) & (j == 0))
    def _():
        key_row = lax.broadcasted_iota(jnp.int32, (tk, tq), 0)
        query = lax.broadcasted_iota(jnp.int32, (tk, tq), 1)
        vis_scr[0] = jnp.zeros((tk, tq), F32)
        vis_scr[1] = jnp.where(key_row < (query // CHUNK + 1) * CHUNK, 0.0, -jnp.inf)
        vis_scr[2] = jnp.full((tk, tq), -jnp.inf, F32)

    n_pairs = (nkb + 1) // 2

    def score_body(t, carry):
        for kb in (2 * t, 2 * t + 1):
            off = pl.multiple_of(kb * tk, tk)
            kit = ki_ref[pl.ds(off, tk), :]
            s = vis_scr[jnp.where(kb >= nkb - 1, 1, 0) + jnp.where(kb >= nkb, 1, 0)]
            for h in range(IDX_HEADS):
                d = jnp.dot(kit, qit_ref[h * IDX_DIM:(h + 1) * IDX_DIM, :], preferred_element_type=F32)
                s = s + wit_ref[h:h + 1, :] * jnp.maximum(d, 0.0)
            bits = pltpu.bitcast(s, jnp.int32)
            key = bits ^ ((bits >> 31) & 0x7FFFFFFF)
            hi_scr[kb] = (key >> 16).astype(i16)
            d1_scr[kb] = ((key >> 8) & 0xFF).astype(i16)
            d0_scr[kb] = (key & 0xFF).astype(i16)
        return carry

    lax.fori_loop(0, n_pairs, score_body, 0)

    one_bf, zero_bf = jnp.ones((), BF16), jnp.zeros((), BF16)

    def counter(digit_scr):
        def count_ge(cand):
            cand16 = cand.astype(i16)

            def body(t, part):
                tiles = []
                for kb in (2 * t, 2 * t + 1):
                    hit = jnp.where(digit_scr[kb] >= cand16, one_bf, zero_bf)
                    tiles += [hit[r * BF16_ROWS:(r + 1) * BF16_ROWS] for r in range(tk // BF16_ROWS)]
                while len(tiles) > 1:
                    tiles = [a + b for a, b in zip(tiles[::2], tiles[1::2])]
                return part + tiles[0].astype(F32)

            part = lax.fori_loop(0, n_pairs, body, jnp.zeros((BF16_ROWS, tq), F32))
            return jnp.sum(part, axis=0, keepdims=True)

        return count_ge

    def refine(dst_scr, upper_scr, upper_thr):
        thr16 = upper_thr.astype(i16)

        def body(t, carry):
            for kb in (2 * t, 2 * t + 1):
                up = upper_scr[kb]
                dst_scr[kb] = jnp.where(up > thr16, i16(256), jnp.where(up == thr16, dst_scr[kb], i16(-1)))
            return carry

        lax.fori_loop(0, n_pairs, body, 0)

    all_hi, all_d1, all_d0 = KEY_ABOVE_NEG_INF >> 16, (KEY_ABOVE_NEG_INF >> 8) & 0xFF, KEY_ABOVE_NEG_INF & 0xFF
    cnt_all = jnp.zeros((1, tq), F32)
    thr_hi, cnt = _kth_largest_key(counter(hi_scr), f_sel, bits=16, lowest=-2 ** 15, cnt_lowest=cnt_all)
    thr_hi = jnp.where(need_sel, thr_hi, all_hi)
    refine(d1_scr, hi_scr, thr_hi)
    thr_d1, cnt = _kth_largest_key(counter(d1_scr), f_sel, bits=8, lowest=0, cnt_lowest=cnt)
    thr_d1 = jnp.where(need_sel, thr_d1, all_d1)
    refine(d0_scr, d1_scr, thr_d1)
    count_d0 = counter(d0_scr)
    thr_d0, cnt = _kth_largest_key(count_d0, f_sel, bits=8, lowest=0, cnt_lowest=cnt)
    thr_d0 = jnp.where(need_sel, thr_d0, all_d0)

    tie = need_sel & (cnt != f_sel)
    any_tie = jnp.max(jnp.where(tie, 1.0, 0.0)) > 0.0

    @pl.when(any_tie)
    def _():
        room = f_sel - count_d0(thr_d0 + 1)
        rr = lax.broadcasted_iota(jnp.int32, (tk, tk), 0)
        cc = lax.broadcasted_iota(jnp.int32, (tk, tk), 1)
        lower = jnp.where(rr >= cc, 1.0, 0.0).astype(BF16)
        thr16 = thr_d0.astype(i16)
        demoted = (thr_d0 - 1).astype(i16)

        def body(kb, seen):
            t = d0_scr[kb]
            eq = t == thr16
            rank = jnp.dot(lower, jnp.where(eq, one_bf, zero_bf), preferred_element_type=F32) + seen
            drop = jnp.where(tie & (rank > room), 1.0, 0.0).astype(BF16) > zero_bf
            d0_scr[kb] = jnp.where(eq & drop, demoted, t)
            return rank[tk - 1:tk, :]

        lax.fori_loop(0, nkb, body, jnp.zeros((1, tq), F32))

    sel16 = thr_d0.astype(i16)

    m_scr[...] = jnp.full(m_scr.shape, NEG_LOGIT, F32)
    acc_scr[...] = jnp.zeros(acc_scr.shape, F32)
    ones_rows = jnp.ones((acc_scr.shape[1] - HEAD_DIM, tk), BF16)

    def set_mask_tile(kb):
        mb_scr[...] = jnp.where(d0_scr[kb] >= sel16, zero_bf, jnp.asarray(NEG_LOGIT, BF16)).astype(F32)

    def produce(kb, h, dst_scr, near=None):
        off = pl.multiple_of(kb * tk, tk)
        g = h // GROUP
        kg = k_ref[pl.ds(off, tk), g * HEAD_DIM:(g + 1) * HEAD_DIM]
        s = jnp.dot(kg, qt_ref[h * HEAD_DIM:(h + 1) * HEAD_DIM, :], preferred_element_type=F32) + mb_scr[...]
        dst_scr[h] = s if near is None else s + bias_ref[near, h]

    def consume(kb, h, src_scr):
        g = h // GROUP
        ps, alphas = [], []
        for c in range(tq // V7X_LANES):
            lanes = slice(c * V7X_LANES, (c + 1) * V7X_LANES)
            m_old = m_scr[h, :, lanes]
            m_new = jnp.maximum(m_old, jnp.max(src_scr[h, :, lanes], axis=0, keepdims=True))
            alphas.append(jnp.exp2(m_old - m_new))
            ps.append(jnp.exp2(src_scr[h, :, lanes] - m_new).astype(BF16))
            m_scr[h, :, lanes] = m_new
        vg = jnp.concatenate([vt_ref[kb, g * HEAD_DIM:(g + 1) * HEAD_DIM, :], ones_rows], axis=0)
        pv = jnp.dot(vg, jnp.concatenate(ps, axis=1), preferred_element_type=F32)
        acc_scr[h] = jnp.concatenate(alphas, axis=1) * acc_scr[h] + pv

    n_far = jnp.maximum(nkb - 2, 0)

    def far_step(kb, cur_scr, nxt_scr):
        if nxt_scr is None:
            for h in range(N_HEADS):
                consume(kb, h, cur_scr)
            return
        kb_next = jnp.minimum(kb + 1, n_far - 1)

        @pl.when(kb_next >= 0)
        def _():
            set_mask_tile(kb_next)

        for h in range(-FAR_LOOKAHEAD, N_HEADS):
            if h + FAR_LOOKAHEAD < N_HEADS:
                produce(kb_next, h + FAR_LOOKAHEAD, nxt_scr)
            if h >= 0:
                consume(kb, h, cur_scr)

    @pl.when(n_far > 0)
    def _():
        set_mask_tile(0)
        for h in range(N_HEADS):
            produce(0, h, sa_scr)

    def pair_body(t, carry):
        far_step(2 * t, sa_scr, sb_scr)
        far_step(2 * t + 1, sb_scr, sa_scr)
        return carry

    lax.fori_loop(0, n_far // 2, pair_body, 0)

    @pl.when(n_far % 2 == 1)
    def _():
        far_step(n_far - 1, sa_scr, None)

    def near_body(kb, carry):
        near = kb - (nkb - 2)
        set_mask_tile(kb)
        for h in range(-NEAR_LOOKAHEAD, N_HEADS):
            if h + NEAR_LOOKAHEAD < N_HEADS:
                produce(kb, h + NEAR_LOOKAHEAD, sa_scr, near)
            if h >= 0:
                consume(kb, h, sa_scr)
        return carry

    lax.fori_loop(n_far, nkb, near_body, 0)

    def normalised(h):
        return acc_scr[h, :HEAD_DIM, :] / acc_scr[h, HEAD_DIM:HEAD_DIM + 1, :]

    for h in range(0, N_HEADS, 2):
        pair = jnp.concatenate([normalised(h), normalised(h + 1)], axis=0)
        o_ref[:, h * HEAD_DIM:(h + 2) * HEAD_DIM] = pair.T.astype(BF16)


def _dsa_cols(qt, qit, wit, k, vt, ki, bias, *, nb, seq, tq, n_sel):
    tk = tq
    nq = seq // tq
    assert tk >= REL_MAX_DIST and seq % tq == 0
    qmap = lambda b, j: (0, b * nq + j)
    once = lambda shape, imap: pl.BlockSpec(shape, imap, pipeline_mode=pl.Buffered(1))
    scratch = [pltpu.VMEM((nq, tk, tq), jnp.int16),
               pltpu.VMEM((nq, tk, tq), jnp.int16),
               pltpu.VMEM((nq, tk, tq), jnp.int16),
               pltpu.VMEM((3, tk, tq), F32),
               pltpu.VMEM((N_HEADS, 1, tq), F32),
               pltpu.VMEM((N_HEADS, HEAD_DIM + BF16_ROWS, tq), F32),
               pltpu.VMEM((tk, tq), F32),
               pltpu.VMEM((N_HEADS, tk, tq), F32),
               pltpu.VMEM((N_HEADS, tk, tq), F32)]
    est = (seq * (2 * ATT_KV + V7X_LANES) * 2 + 3 * seq * tq * 2 + 4 * N_HEADS * tq * tk * 4
           + 2 * tq * (2 * ATT_Q + IDX_Q) * 2 + N_HEADS * HEAD_DIM * tq * 4
           + IDX_HEADS * tq * tk * 4 * 2 + 8 * tq * tk * 4)
    return pl.pallas_call(
        functools.partial(_dsa_cols_kernel, tq=tq, tk=tk, n_sel=n_sel),
        out_shape=jax.ShapeDtypeStruct((nb, seq, ATT_Q), BF16),
        grid=(nb, nq),
        in_specs=[pl.BlockSpec((ATT_Q, tq), qmap), pl.BlockSpec((IDX_Q, tq), qmap),
                  pl.BlockSpec((IDX_HEADS, tq), qmap),
                  once((None, seq, ATT_KV), lambda b, j: (b, 0, 0)),
                  once((None, nq, ATT_KV, tk), lambda b, j: (b, 0, 0, 0)),
                  once((None, seq, IDX_DIM), lambda b, j: (b, 0, 0)),
                  _resident(bias.shape)],
        out_specs=pl.BlockSpec((None, tq, ATT_Q), lambda b, j: (b, j, 0)),
        scratch_shapes=scratch,
        compiler_params=pltpu.CompilerParams(dimension_semantics=("arbitrary", "arbitrary"),
                                             vmem_limit_bytes=_vmem_limit(est)),
        name="dsa_cols",
    )(qt, qit, wit, k, vt, ki, bias)


def _dsa_rows_kernel(q_ref, qi_ref, wi_ref, k_ref, v_ref, ki_ref, bias_ref, o_ref,
                     keys_scr, qstk_scr, qistk_scr, wib_scr, *, tq, tk, n_sel, n_keys):
    f_sel = float(n_sel)
    nkb = -(-n_keys // tk)
    n_vis = jnp.full((tq, 1), n_keys, jnp.int32)

    for hh in range(N_HEADS):
        g, i = divmod(hh, GROUP)
        qstk_scr[g, i * tq:(i + 1) * tq, :] = q_ref[:, hh * HEAD_DIM:(hh + 1) * HEAD_DIM]
    for h in range(IDX_HEADS):
        qistk_scr[h * tq:(h + 1) * tq, :] = qi_ref[:, h * IDX_DIM:(h + 1) * IDX_DIM]
        wib_scr[h] = jnp.broadcast_to(wi_ref[:, h:h + 1], (tq, tk))

    nt_dims = (((1,), (1,)), ((), ()))

    def score_body(kb, carry):
        off = pl.multiple_of(kb * tk, tk)
        d = lax.dot_general(qistk_scr[...], ki_ref[pl.ds(off, tk), :], nt_dims, preferred_element_type=F32)
        s = jnp.zeros((tq, tk), F32)
        for h in range(IDX_HEADS):
            s = s + wib_scr[h] * jnp.maximum(d[h * tq:(h + 1) * tq], 0.0)
        kpos = off + lax.broadcasted_iota(jnp.int32, (tq, tk), 1)
        keys_scr[kb] = jnp.where(kpos < n_vis, _sortable_key(s), INT_MIN)
        return carry

    lax.fori_loop(0, nkb, score_body, 0)

    def count_ge(cand):
        candb = jnp.broadcast_to(cand, (tq, V7X_LANES))

        def body(kb, part):
            t = keys_scr[kb]
            for c in range(tk // V7X_LANES):
                part = part + jnp.where(t[:, c * V7X_LANES:(c + 1) * V7X_LANES] >= candb, 1.0, 0.0)
            return part

        part = lax.fori_loop(0, nkb, body, jnp.zeros((tq, V7X_LANES), F32))
        return jnp.sum(part, axis=1, keepdims=True)

    thr, cnt_thr = _kth_largest_key(count_ge, f_sel, bits=32, lowest=INT_MIN,
                                    cnt_lowest=jnp.zeros((tq, 1), F32))

    tie = (n_vis > n_sel) & (cnt_thr != f_sel)
    any_tie = jnp.max(jnp.where(tie, 1.0, 0.0)) > 0.0

    @pl.when(any_tie)
    def _():
        room = f_sel - count_ge(thr + 1)
        rr = lax.broadcasted_iota(jnp.int32, (tk, tk), 0)
        cc = lax.broadcasted_iota(jnp.int32, (tk, tk), 1)
        upper = jnp.where(rr <= cc, 1.0, 0.0).astype(BF16)

        def body(kb, seen):
            t = keys_scr[kb]
            eq = t == thr
            eqf = jnp.where(eq, 1.0, 0.0)
            rank = jnp.dot(eqf.astype(BF16), upper, preferred_element_type=F32) + seen
            keys_scr[kb] = jnp.where(eq & tie & (rank > room), thr - 1, t)
            return seen + jnp.sum(eqf, axis=1, keepdims=True)

        lax.fori_loop(0, nkb, body, jnp.zeros((tq, 1), F32))

    thr_sel = jnp.maximum(thr, INT_MIN + 1)

    lp = nkb * tk
    mask_add = jnp.concatenate([jnp.where(keys_scr[kb] >= thr_sel, 0.0, NEG_LOGIT) for kb in range(nkb)], axis=1)
    far = jnp.zeros((GROUP, tq, lp - 2 * tk), F32)
    for g in range(KV_HEADS):
        cols = slice(g * HEAD_DIM, (g + 1) * HEAD_DIM)
        heads = slice(g * GROUP, (g + 1) * GROUP)
        s = lax.dot_general(qstk_scr[g], k_ref[:, cols], nt_dims, preferred_element_type=F32)
        s = s.reshape(GROUP, tq, lp) + mask_add[None]
        s = s + jnp.concatenate([far, bias_ref[0, heads], bias_ref[1, heads]], axis=-1)
        p = jnp.exp2(s - jnp.max(s, axis=-1, keepdims=True))
        l = jnp.sum(p, axis=-1, keepdims=True)
        pv = jnp.dot(p.astype(BF16).reshape(GROUP * tq, lp), v_ref[:, cols], preferred_element_type=F32)
        o = pv.reshape(GROUP, tq, HEAD_DIM) / l
        for i in range(GROUP):
            hh = g * GROUP + i
            o_ref[:, hh * HEAD_DIM:(hh + 1) * HEAD_DIM] = o[i].astype(BF16)


def _dsa_rows(q, qi, wi, k, v, ki, bias, *, tq, tk, n_sel, q_start, n_keys):
    nb = q.shape[0]
    lp = k.shape[1]
    nkb = lp // tk
    qmap = lambda b: (b, 0, 0)
    assert tk >= REL_MAX_DIST and lp % tk == 0 and q_start == (nkb - 1) * tk and nkb == -(-n_keys // tk)
    scratch = [pltpu.VMEM((nkb, tq, tk), jnp.int32),
               pltpu.VMEM((KV_HEADS, GROUP * tq, HEAD_DIM), BF16),
               pltpu.VMEM((IDX_HEADS * tq, IDX_DIM), BF16),
               pltpu.VMEM((IDX_HEADS, tq, tk), F32)]
    est = (2 * lp * (2 * ATT_KV + V7X_LANES) * 2 + lp * tq * 4 + 2 * N_HEADS * tq * tk * 4
           + IDX_HEADS * tq * tk * 4 * 3 + 8 * N_HEADS * tq * lp * 4)
    return pl.pallas_call(
        functools.partial(_dsa_rows_kernel, tq=tq, tk=tk, n_sel=n_sel, n_keys=n_keys),
        out_shape=jax.ShapeDtypeStruct((nb, tq, ATT_Q), BF16),
        grid=(nb,),
        in_specs=[pl.BlockSpec((None, tq, ATT_Q), qmap), pl.BlockSpec((None, tq, IDX_Q), qmap),
                  pl.BlockSpec((None, tq, V7X_LANES), qmap),
                  pl.BlockSpec((None, lp, ATT_KV), qmap), pl.BlockSpec((None, lp, ATT_KV), qmap),
                  pl.BlockSpec((None, lp, IDX_DIM), qmap),
                  _resident(bias.shape)],
        out_specs=pl.BlockSpec((None, tq, ATT_Q), qmap),
        scratch_shapes=scratch,
        compiler_params=pltpu.CompilerParams(dimension_semantics=("arbitrary",),
                                             vmem_limit_bytes=_vmem_limit(est)),
        name="dsa_rows",
    )(q, qi, wi, k, v, ki, bias)


def _merge_ffn_kernel(a_ref, gate_ref, ga_ref, gb_ref, h_ref, wa_ref, wb_ref, wo_ref, g2_ref, b2_ref,
                      wg_ref, wu_ref, wd_ref, g3_ref, b3_ref, o_ref, act_scr, *, alpha):
    br_a = jnp.dot(a_ref[...], wa_ref[...], preferred_element_type=F32)
    br_b = jnp.dot(gate_ref[...], wb_ref[...], preferred_element_type=F32)
    merge = jax.nn.sigmoid(ga_ref[...]) * br_a + jax.nn.sigmoid(gb_ref[...]) * br_b
    mixed = jnp.dot(merge.astype(BF16), wo_ref[...], preferred_element_type=F32)
    h2 = _layer_norm(alpha * h_ref[...] + mixed, g2_ref[...], b2_ref[...])
    y = _swiglu(h2.astype(BF16), wg_ref, wu_ref, wd_ref, act_scr)
    o_ref[...] = _layer_norm(alpha * h2 + 0.5 * y, g3_ref[...], b3_ref[...])


def _merge_ffn(a, gate, ga, gb, h, wa, wb, wo, g2, b2, wg, wu, wd, g3, b3, *, tm, alpha):
    m = h.shape[0]
    assert m % tm == 0
    row = lambda i: (i, 0)
    tile = lambda: pl.BlockSpec((tm, D_MODEL), row)
    vec = lambda: _resident((1, D_MODEL))
    sq = lambda: _resident((D_MODEL, D_MODEL))
    weights = (3 * D_MODEL * D_MODEL + 3 * D_MODEL * D_FF) * 2
    tiles = tm * D_MODEL * 4 * 12 + tm * D_FF * (2 + 3 * 4)
    return pl.pallas_call(
        functools.partial(_merge_ffn_kernel, alpha=alpha),
        out_shape=jax.ShapeDtypeStruct((m, D_MODEL), F32),
        grid=(m // tm,),
        in_specs=[tile(), tile(), tile(), tile(), tile(), sq(), sq(), sq(), vec(), vec(),
                  _resident((D_MODEL, D_FF)), _resident((D_MODEL, D_FF)), _resident((D_FF, D_MODEL)),
                  vec(), vec()],
        out_specs=tile(),
        scratch_shapes=[pltpu.VMEM((tm, D_FF), BF16)],
        compiler_params=pltpu.CompilerParams(dimension_semantics=("arbitrary",),
                                             vmem_limit_bytes=_vmem_limit(weights + tiles)),
        name="merge_ffn",
    )(a, gate, ga, gb, h, wa, wb, wo, g2, b2, wg, wu, wd, g3, b3)


def _split_w_in(w_in):
    o_k = ATT_Q
    o_qi = ATT_Q + 2 * ATT_KV
    o_ki = o_qi + IDX_Q
    o_wi = o_ki + IDX_DIM
    o_zg = o_wi + IDX_HEADS
    o_ga = o_zg + 2 * D_MODEL
    o_gb = o_ga + D_MODEL
    wb = w_in.astype(BF16)
    pad = lambda w, n: jnp.pad(w, ((0, 0), (0, n - w.shape[1])))
    widx = jnp.concatenate([wb[:, o_qi:o_ki], pad(wb[:, o_ki:o_wi], V7X_LANES),
                            pad(wb[:, o_wi:o_zg], V7X_LANES)], axis=1)
    return wb[:, :o_qi], widx, wb[:, o_zg:o_ga], wb[:, o_ga:o_gb], wb[:, o_gb:]


def kernel(x_prompt, x_sample, cache_k, cache_v, cache_kidx, rel_table, ln1_g, ln1_b, ffn1_wg, ffn1_wu, ffn1_wd, w_in, gm_ln_g, gm_ln_b, gm_ws, gm_bs, w_br_a, w_br_b, w_out, ln2_g, ln2_b, ffn2_wg, ffn2_wu, ffn2_wd, ln3_g, ln3_b):
    depth = ln1_g.shape[0]
    assert depth == 1, "single-layer step"
    alpha = (2 * depth) ** 0.25
    nb, seq, _ = x_prompt.shape
    ns, n_new, _ = x_sample.shape
    past = cache_k.shape[2]
    total = past + n_new
    l = 0
    vec = lambda p: p[l].reshape(1, D_MODEL)
    bf = lambda p: p[l].astype(BF16)

    w_qkv, w_idx, w_zg, w_ga, w_gb = _split_w_in(w_in[l])
    f1 = (bf(ffn1_wg), bf(ffn1_wu), bf(ffn1_wd))
    f2 = (bf(ffn2_wg), bf(ffn2_wu), bf(ffn2_wd))
    br = (bf(w_br_a), bf(w_br_b), bf(w_out))

    def dense_in(x, tm, gate_rows, emit_vg, kv_block):
        h = _ffn_ln(x, *f1, vec(ln1_g), vec(ln1_b), tm=tm, alpha=alpha)
        reps = GM_CHUNK // gate_rows
        ws = jnp.tile(gm_ws[l][:, :gate_rows, :gate_rows], (1, reps, reps))
        bst = jnp.tile(gm_bs[l][:, :gate_rows].T, (reps, 1))
        parts = _mix_in(h, w_qkv, w_idx, w_zg, w_ga, w_gb, vec(gm_ln_g), vec(gm_ln_b), ws, bst,
                        tm=tm, gate_rows=gate_rows, emit_vg=emit_vg, kv_block=kv_block)
        return h, parts

    def dense_out(a, p, h, tm):
        return _merge_ffn(a, p["gate"], p["ga"], p["gb"], h, *br, vec(ln2_g), vec(ln2_b), *f2,
                          vec(ln3_g), vec(ln3_b), tm=tm, alpha=alpha)

    mp = nb * seq
    tq = 256
    hp, pp = dense_in(x_prompt.reshape(mp, D_MODEL), 512, GM_CHUNK, False, tq)
    bias_p = _bias_tiles(rel_table, tq, tq, 0)
    a = _dsa_cols(pp["q"], pp["qi"], pp["wi"], pp["k"].reshape(nb, seq, ATT_KV),
                  pp["v"].reshape(nb, seq // tq, ATT_KV, tq), pp["ki"].reshape(nb, seq, IDX_DIM), bias_p,
                  nb=nb, seq=seq, tq=tq, n_sel=min(TOPK_MAX, seq // 4))
    y_p = dense_out(a.reshape(mp, ATT_Q), pp, hp, 512).reshape(nb, seq, D_MODEL)

    ms = ns * n_new
    hs, ps = dense_in(x_sample.reshape(ms, D_MODEL), 128, n_new, True, None)
    tk_s = V7X_LANES
    lp = -(-total // tk_s) * tk_s
    cat = lambda c, new, w: jnp.pad(
        jnp.concatenate([c[l].reshape(ns, past, w).astype(BF16), new.reshape(ns, n_new, w)], axis=1),
        ((0, 0), (0, lp - total), (0, 0)))
    bias_s = _bias_tiles(rel_table, n_new, tk_s, 1)
    a = _dsa_rows(ps["q"].reshape(ns, n_new, ATT_Q), ps["qi"].reshape(ns, n_new, IDX_Q),
                  ps["wi"].reshape(ns, n_new, V7X_LANES),
                  cat(cache_k, ps["k"], ATT_KV), cat(cache_v, ps["v"], ATT_KV), cat(cache_kidx, ps["ki"], IDX_DIM),
                  bias_s, tq=n_new, tk=tk_s, n_sel=min(TOPK_MAX, total // 4), q_start=past, n_keys=total)
    y_s = dense_out(a.reshape(ms, ATT_Q), ps, hs, 128).reshape(ns, n_new, D_MODEL)

    kv5 = lambda x, b, t: x.reshape(1, b, t, KV_HEADS, HEAD_DIM)
    return (y_p, y_s,
            kv5(pp["k_f32"], nb, seq), kv5(pp["v_f32"], nb, seq), pp["ki_f32"].reshape(1, nb, seq, IDX_DIM),
            kv5(ps["k_f32"], ns, n_new), kv5(ps["v_f32"], ns, n_new), ps["ki_f32"].reshape(1, ns, n_new, IDX_DIM),
            ps["vg"].reshape(1, ns, n_new, D_MODEL))
```

```python
import functools
import math

import numpy as np
import jax
import jax.numpy as jnp
from jax import lax
from jax.experimental import pallas as pl
from jax.experimental.pallas import tpu as pltpu

D_MODEL = 1024
CHUNK = 64
N_HEADS = 16
HEAD_DIM = 64
KV_HEADS = 4
GROUP = N_HEADS // KV_HEADS
IDX_HEADS = 8
IDX_DIM = 64
TOPK_MAX = 256
GM_CHUNK = 128
GM_GROUPS = 4
GM_GROUP_DIM = D_MODEL // GM_GROUPS
D_FF = 2816
REL_BUCKETS = 32
REL_MAX_DIST = 128
LN_EPS = 1e-5
ATT_Q = N_HEADS * HEAD_DIM
ATT_KV = KV_HEADS * HEAD_DIM
IDX_Q = IDX_HEADS * IDX_DIM

V7X_LANES = 128
V7X_SUBLANES = 8
BF16_ROWS = 2 * V7X_SUBLANES
FAR_LOOKAHEAD = 1
NEAR_LOOKAHEAD = 4
V7X_MXU_DIM = 256
V7X_VMEM_BYTES = 64 * 1024 * 1024

FF_BLOCK = V7X_MXU_DIM
LOG2_E = math.log2(math.e)
Q_SCALE = LOG2_E * HEAD_DIM ** -0.5
NEG_LOGIT = -1e30
INT_MIN = -2 ** 31
KEY_ABOVE_NEG_INF = -(0x7F800000)
F32 = jnp.float32
BF16 = jnp.bfloat16


def _vmem_limit(nbytes):
    return int(min(max(nbytes, 16 * 1024 * 1024), V7X_VMEM_BYTES - 6 * 1024 * 1024))


def _resident(shape):
    zeros = (0,) * len(shape)
    return pl.BlockSpec(shape, lambda *_: zeros, pipeline_mode=pl.Buffered(1))


def _layer_norm(x, g, b):
    mu = jnp.mean(x, axis=-1, keepdims=True)
    xc = x - mu
    var = jnp.mean(xc * xc, axis=-1, keepdims=True)
    return xc * lax.rsqrt(var + LN_EPS) * g + b


def _swiglu(xb, wg_ref, wu_ref, wd_ref, act_scr):
    for c in range(D_FF // FF_BLOCK):
        cols = slice(c * FF_BLOCK, (c + 1) * FF_BLOCK)
        g = jnp.dot(xb, wg_ref[:, cols], preferred_element_type=F32)
        u = jnp.dot(xb, wu_ref[:, cols], preferred_element_type=F32)
        act_scr[:, cols] = (g * jax.nn.sigmoid(g) * u).astype(BF16)
    return jnp.dot(act_scr[...], wd_ref[...], preferred_element_type=F32)


def _sortable_key(score):
    score = jnp.where(score == 0.0, 0.0, score)
    bits = pltpu.bitcast(score, jnp.int32)
    return bits ^ ((bits >> 31) & 0x7FFFFFFF)


def _kth_largest_key(count_ge, f_sel, *, bits, lowest, cnt_lowest):
    def bit_body(i, carry):
        prefix, cnt_t = carry
        cand = prefix + lax.shift_left(jnp.int32(1), jnp.int32(bits - 1) - i)
        cnt = count_ge(cand)
        take = cnt >= f_sel
        return jnp.where(take, cand, prefix), jnp.where(take, cnt, cnt_t)

    return lax.fori_loop(0, bits, bit_body, (jnp.full(cnt_lowest.shape, lowest, jnp.int32), cnt_lowest))


def _ffn_ln_kernel(x_ref, wg_ref, wu_ref, wd_ref, g_ref, b_ref, o_ref, act_scr, *, alpha):
    x = x_ref[...]
    y = _swiglu(x.astype(BF16), wg_ref, wu_ref, wd_ref, act_scr)
    o_ref[...] = _layer_norm(alpha * x + 0.5 * y, g_ref[...], b_ref[...])


def _ffn_ln(x, wg, wu, wd, g, b, *, tm, alpha):
    m = x.shape[0]
    assert m % tm == 0
    row = lambda i: (i, 0)
    weights = 3 * D_MODEL * D_FF * 2
    tiles = tm * D_MODEL * 4 * 4 + tm * D_FF * (2 + 3 * 4)
    return pl.pallas_call(
        functools.partial(_ffn_ln_kernel, alpha=alpha),
        out_shape=jax.ShapeDtypeStruct((m, D_MODEL), F32),
        grid=(m // tm,),
        in_specs=[pl.BlockSpec((tm, D_MODEL), row),
                  _resident((D_MODEL, D_FF)), _resident((D_MODEL, D_FF)), _resident((D_FF, D_MODEL)),
                  _resident((1, D_MODEL)), _resident((1, D_MODEL))],
        out_specs=pl.BlockSpec((tm, D_MODEL), row),
        scratch_shapes=[pltpu.VMEM((tm, D_FF), BF16)],
        compiler_params=pltpu.CompilerParams(dimension_semantics=("arbitrary",),
                                             vmem_limit_bytes=_vmem_limit(weights + tiles)),
        name="ffn_ln",
    )(x, wg, wu, wd, g, b)


def _mix_in_kernel(h_ref, wqkv_ref, widx_ref, wzg_ref, wga_ref, wgb_ref, gmg_ref, gmb_ref, ws_ref, bst_ref,
                   *out_refs, names, tm, gate_rows, kv_block):
    o = dict(zip(names, out_refs))
    hb = h_ref[...].astype(BF16)

    zg = jnp.dot(hb, wzg_ref[...], preferred_element_type=F32)

    qkv = jnp.dot(hb, wqkv_ref[...], preferred_element_type=F32)
    q = qkv[:, :ATT_Q] * Q_SCALE
    k = qkv[:, ATT_Q:ATT_Q + ATT_KV]
    v = qkv[:, ATT_Q + ATT_KV:]
    o["k_f32"][...] = k
    o["v_f32"][...] = v
    o["k"][...] = k.astype(BF16)

    idx = jnp.dot(hb, widx_ref[...], preferred_element_type=F32)
    qi = idx[:, :IDX_Q]
    ki = idx[:, IDX_Q:IDX_Q + IDX_DIM]
    wi = idx[:, IDX_Q + V7X_LANES:]
    o["ki_f32"][...] = ki
    o["ki"][...] = ki.astype(BF16)

    if kv_block is None:
        o["q"][...] = q.astype(BF16)
        o["v"][...] = v.astype(BF16)
        o["qi"][...] = qi.astype(BF16)
        o["wi"][...] = wi
    else:
        o["q"][...] = q.T.astype(BF16)
        o["qi"][...] = qi.T.astype(BF16)
        o["wi"][...] = wi.T[:IDX_HEADS]
        vt = v.T.astype(BF16)
        for c in range(tm // kv_block):
            o["v"][c] = vt[:, c * kv_block:(c + 1) * kv_block]

    o["ga"][...] = jnp.dot(hb, wga_ref[...], preferred_element_type=F32)
    o["gb"][...] = jnp.dot(hb, wgb_ref[...], preferred_element_type=F32)

    zg = jax.nn.gelu(zg)
    u = zg[:, :D_MODEL]
    vg = _layer_norm(zg[:, D_MODEL:], gmg_ref[...], gmb_ref[...])
    if "vg" in o:
        o["vg"][...] = vg
    vgb = vg.astype(BF16)

    r = lax.broadcasted_iota(jnp.int32, (GM_CHUNK, GM_CHUNK), 0)
    c = lax.broadcasted_iota(jnp.int32, (GM_CHUNK, GM_CHUNK), 1)
    keep = (r >= c) & ((r // gate_rows) == (c // gate_rows))
    for grp in range(GM_GROUPS):
        cols = slice(grp * GM_GROUP_DIM, (grp + 1) * GM_GROUP_DIM)
        w = jnp.where(keep, ws_ref[grp], 0.0).astype(BF16)
        bias = bst_ref[:, grp:grp + 1]
        for ch in range(tm // GM_CHUNK):
            rows = slice(ch * GM_CHUNK, (ch + 1) * GM_CHUNK)
            s = jnp.dot(w, vgb[rows, cols], preferred_element_type=F32) + bias
            o["gate"][rows, cols] = (u[rows, cols] * s).astype(BF16)


def _mix_in(h, wqkv, widx, wzg, wga, wgb, gmg, gmb, ws, bst, *, tm, gate_rows, emit_vg, kv_block):
    m = h.shape[0]
    assert m % tm == 0 and tm % GM_CHUNK == 0
    row = lambda i: (i, 0)
    col = lambda i: (0, i)
    rows = lambda n, dt: (jax.ShapeDtypeStruct((m, n), dt), pl.BlockSpec((tm, n), row))
    cols = lambda n, dt: (jax.ShapeDtypeStruct((n, m), dt), pl.BlockSpec((n, tm), col))
    outs = {"k_f32": rows(ATT_KV, F32), "v_f32": rows(ATT_KV, F32), "ki_f32": rows(IDX_DIM, F32),
            "k": rows(ATT_KV, BF16), "ki": rows(IDX_DIM, BF16),
            "gate": rows(D_MODEL, BF16), "ga": rows(D_MODEL, F32), "gb": rows(D_MODEL, F32)}
    if kv_block is None:
        outs.update(q=rows(ATT_Q, BF16), v=rows(ATT_KV, BF16), qi=rows(IDX_Q, BF16), wi=rows(V7X_LANES, F32))
    else:
        assert tm % kv_block == 0
        outs.update(q=cols(ATT_Q, BF16), qi=cols(IDX_Q, BF16), wi=cols(IDX_HEADS, F32),
                    v=(jax.ShapeDtypeStruct((m // kv_block, ATT_KV, kv_block), BF16),
                       pl.BlockSpec((tm // kv_block, ATT_KV, kv_block), lambda i: (i, 0, 0))))
    if emit_vg:
        outs["vg"] = rows(D_MODEL, F32)
    names = tuple(outs)
    n_in = wqkv.shape[1] + widx.shape[1] + wzg.shape[1] + 2 * D_MODEL
    weights = D_MODEL * n_in * 2
    tiles = tm * (n_in * 4 * 3 + D_MODEL * 4 * 2)
    res = pl.pallas_call(
        functools.partial(_mix_in_kernel, names=names, tm=tm, gate_rows=gate_rows, kv_block=kv_block),
        out_shape=[outs[n][0] for n in names],
        grid=(m // tm,),
        in_specs=[pl.BlockSpec((tm, D_MODEL), row),
                  _resident(wqkv.shape), _resident(widx.shape), _resident(wzg.shape),
                  _resident(wga.shape), _resident(wgb.shape),
                  _resident((1, D_MODEL)), _resident((1, D_MODEL)),
                  _resident(ws.shape), _resident(bst.shape)],
        out_specs=[outs[n][1] for n in names],
        compiler_params=pltpu.CompilerParams(dimension_semantics=("arbitrary",),
                                             vmem_limit_bytes=_vmem_limit(weights + tiles)),
        name="mix_in",
    )(h, wqkv, widx, wzg, wga, wgb, gmg, gmb, ws, bst)
    return dict(zip(names, res))


def _bucket_thresholds():
    half = REL_BUCKETS // 2
    max_exact = half // 2
    n = np.arange(1, 2 * REL_MAX_DIST, dtype=np.float32)
    large = max_exact + (np.log(n / np.float32(max_exact)) / np.float32(math.log(REL_MAX_DIST / max_exact))
                         * np.float32(half - max_exact)).astype(np.int32)
    large = np.minimum(large, half - 1)
    thr = [int(np.argmax(large >= b)) + 1 for b in range(max_exact + 1, half)]
    return max_exact, half, thr


def _bias_kernel(table_ref, o_ref, *, shape, key_axis, tk):
    max_exact, half, thr = _bucket_thresholds()
    d = pl.program_id(0)
    h = pl.program_id(1)
    rel = (lax.broadcasted_iota(jnp.int32, shape, key_axis) - lax.broadcasted_iota(jnp.int32, shape, 1 - key_axis)
           + (d - 1) * tk)
    n = jnp.abs(rel)
    large = jnp.full(shape, max_exact, jnp.int32)
    for t in thr:
        large = large + jnp.where(n >= t, 1, 0)
    bucket = jnp.where(rel > 0, half, 0) + jnp.where(n < max_exact, n, large)
    acc = jnp.zeros(shape, F32)
    for b in range(REL_BUCKETS):
        acc = acc + jnp.where(bucket == b, table_ref[b, h], 0.0)
    o_ref[...] = (acc - table_ref[half - 1, h]) * LOG2_E


def _bias_tiles(rel_table, tq, tk, key_axis):
    shape = (tq, tk) if key_axis == 1 else (tk, tq)
    return pl.pallas_call(
        functools.partial(_bias_kernel, shape=shape, key_axis=key_axis, tk=tk),
        out_shape=jax.ShapeDtypeStruct((2, N_HEADS) + shape, F32),
        grid=(2, N_HEADS),
        in_specs=[pl.BlockSpec(memory_space=pltpu.SMEM)],
        out_specs=pl.BlockSpec((None, None) + shape, lambda d, h: (d, h, 0, 0)),
        compiler_params=pltpu.CompilerParams(dimension_semantics=("arbitrary", "arbitrary")),
        name="rel_bias",
    )(rel_table)


def _dsa_cols_kernel(qt_ref, qit_ref, wit_ref, k_ref, vt_ref, ki_ref, bias_ref, o_ref,
                     hi_scr, d1_scr, d0_scr, vis_scr, m_scr, acc_scr, mb_scr, mb2_scr, sa_scr, sb_scr, *, tq, tk, n_sel):
    f_sel = float(n_sel)
    i16 = jnp.int16
    j = pl.program_id(1)
    nkb = j + 1
    pos = j * tq + lax.broadcasted_iota(jnp.int32, (1, tq), 1)
    n_vis = (pos // CHUNK + 1) * CHUNK
    need_sel = n_vis > n_sel

    @pl.when((pl.program_id(0) == 0) & (j == 0))
    def _():
        key_row = lax.broadcasted_iota(jnp.int32, (tk, tq), 0)
        query = lax.broadcasted_iota(jnp.int32, (tk, tq), 1)
        vis_scr[0] = jnp.zeros((tk, tq), F32)
        vis_scr[1] = jnp.where(key_row < (query // CHUNK + 1) * CHUNK, 0.0, -jnp.inf)
        vis_scr[2] = jnp.full((tk, tq), -jnp.inf, F32)

    n_pairs = (nkb + 1) // 2

    def score_body(t, carry):
        for kb in (2 * t, 2 * t + 1):
            off = pl.multiple_of(kb * tk, tk)
            kit = ki_ref[pl.ds(off, tk), :]
            s = vis_scr[jnp.where(kb >= nkb - 1, 1, 0) + jnp.where(kb >= nkb, 1, 0)]
            for h in range(IDX_HEADS):
                d = jnp.dot(kit, qit_ref[h * IDX_DIM:(h + 1) * IDX_DIM, :], preferred_element_type=F32)
                s = s + wit_ref[h:h + 1, :] * jnp.maximum(d, 0.0)
            bits = pltpu.bitcast(s, jnp.int32)
            key = bits ^ ((bits >> 31) & 0x7FFFFFFF)
            hi_scr[kb] = (key >> 16).astype(i16)
            d1_scr[kb] = ((key >> 8) & 0xFF).astype(i16)
            d0_scr[kb] = (key & 0xFF).astype(i16)
        return carry

    lax.fori_loop(0, n_pairs, score_body, 0)

    one_bf, zero_bf = jnp.ones((), BF16), jnp.zeros((), BF16)

    def counter(digit_scr):
        def count_ge(cand):
            cand16 = cand.astype(i16)

            def body(t, part):
                tiles = []
                for kb in (2 * t, 2 * t + 1):
                    hit = jnp.where(digit_scr[kb] >= cand16, one_bf, zero_bf)
                    tiles += [hit[r * BF16_ROWS:(r + 1) * BF16_ROWS] for r in range(tk // BF16_ROWS)]
                while len(tiles) > 1:
                    tiles = [a + b for a, b in zip(tiles[::2], tiles[1::2])]
                return part + tiles[0].astype(F32)

            part = lax.fori_loop(0, n_pairs, body, jnp.zeros((BF16_ROWS, tq), F32))
            return jnp.sum(part, axis=0, keepdims=True)

        return count_ge

    def refine(dst_scr, upper_scr, upper_thr):
        thr16 = upper_thr.astype(i16)

        def body(t, carry):
            for kb in (2 * t, 2 * t + 1):
                up = upper_scr[kb]
                dst_scr[kb] = jnp.where(up > thr16, i16(256), jnp.where(up == thr16, dst_scr[kb], i16(-1)))
            return carry

        lax.fori_loop(0, n_pairs, body, 0)

    all_hi, all_d1, all_d0 = KEY_ABOVE_NEG_INF >> 16, (KEY_ABOVE_NEG_INF >> 8) & 0xFF, KEY_ABOVE_NEG_INF & 0xFF
    cnt_all = jnp.zeros((1, tq), F32)
    thr_hi, cnt = _kth_largest_key(counter(hi_scr), f_sel, bits=16, lowest=-2 ** 15, cnt_lowest=cnt_all)
    thr_hi = jnp.where(need_sel, thr_hi, all_hi)
    refine(d1_scr, hi_scr, thr_hi)
    thr_d1, cnt = _kth_largest_key(counter(d1_scr), f_sel, bits=8, lowest=0, cnt_lowest=cnt)
    thr_d1 = jnp.where(need_sel, thr_d1, all_d1)
    refine(d0_scr, d1_scr, thr_d1)
    count_d0 = counter(d0_scr)
    thr_d0, cnt = _kth_largest_key(count_d0, f_sel, bits=8, lowest=0, cnt_lowest=cnt)
    thr_d0 = jnp.where(need_sel, thr_d0, all_d0)

    tie = need_sel & (cnt != f_sel)
    any_tie = jnp.max(jnp.where(tie, 1.0, 0.0)) > 0.0

    @pl.when(any_tie)
    def _():
        room = f_sel - count_d0(thr_d0 + 1)
        rr = lax.broadcasted_iota(jnp.int32, (tk, tk), 0)
        cc = lax.broadcasted_iota(jnp.int32, (tk, tk), 1)
        lower = jnp.where(rr >= cc, 1.0, 0.0).astype(BF16)
        thr16 = thr_d0.astype(i16)
        demoted = (thr_d0 - 1).astype(i16)

        def body(kb, seen):
            t = d0_scr[kb]
            eq = t == thr16
            rank = jnp.dot(lower, jnp.where(eq, one_bf, zero_bf), preferred_element_type=F32) + seen
            drop = jnp.where(tie & (rank > room), 1.0, 0.0).astype(BF16) > zero_bf
            d0_scr[kb] = jnp.where(eq & drop, demoted, t)
            return rank[tk - 1:tk, :]

        lax.fori_loop(0, nkb, body, jnp.zeros((1, tq), F32))

    sel16 = thr_d0.astype(i16)

    m_scr[...] = jnp.full(m_scr.shape, NEG_LOGIT, F32)
    acc_scr[...] = jnp.zeros(acc_scr.shape, F32)
    ones_rows = jnp.ones((acc_scr.shape[1] - HEAD_DIM, tk), BF16)

    def set_mask_tile(kb, mask_scr):
        mask_scr[...] = jnp.where(d0_scr[kb] >= sel16, zero_bf, jnp.asarray(NEG_LOGIT, BF16)).astype(F32)

    def produce(kb, h, dst_scr, mask_scr, near=None):
        off = pl.multiple_of(kb * tk, tk)
        g = h // GROUP
        kg = k_ref[pl.ds(off, tk), g * HEAD_DIM:(g + 1) * HEAD_DIM]
        s = jnp.dot(kg, qt_ref[h * HEAD_DIM:(h + 1) * HEAD_DIM, :], preferred_element_type=F32) + mask_scr[...]
        dst_scr[h] = s if near is None else s + bias_ref[near, h]

    def consume(kb, h, src_scr):
        g = h // GROUP
        ps, alphas = [], []
        for c in range(tq // V7X_LANES):
            lanes = slice(c * V7X_LANES, (c + 1) * V7X_LANES)
            m_old = m_scr[h, :, lanes]
            m_new = jnp.maximum(m_old, jnp.max(src_scr[h, :, lanes], axis=0, keepdims=True))
            alphas.append(jnp.exp2(m_old - m_new))
            ps.append(jnp.exp2(src_scr[h, :, lanes] - m_new).astype(BF16))
            m_scr[h, :, lanes] = m_new
        vg = jnp.concatenate([vt_ref[kb, g * HEAD_DIM:(g + 1) * HEAD_DIM, :], ones_rows], axis=0)
        pv = jnp.dot(vg, jnp.concatenate(ps, axis=1), preferred_element_type=F32)
        acc_scr[h] = jnp.concatenate(alphas, axis=1) * acc_scr[h] + pv

    n_far = jnp.maximum(nkb - 2, 0)

    def near_step(kb, near, prefetch_far):
        set_mask_tile(kb, mb_scr)
        if prefetch_far:
            set_mask_tile(0, mb2_scr)
        for h in range(-NEAR_LOOKAHEAD, N_HEADS):
            if h + NEAR_LOOKAHEAD < N_HEADS:
                produce(kb, h + NEAR_LOOKAHEAD, sa_scr, mb_scr, near)
            if h >= 0:
                if prefetch_far:
                    produce(0, h, sb_scr, mb2_scr)
                consume(kb, h, sa_scr)

    @pl.when(nkb >= 2)
    def _():
        near_step(nkb - 2, 0, False)

    @pl.when(n_far > 0)
    def _():
        near_step(nkb - 1, 1, True)

    @pl.when(n_far == 0)
    def _():
        near_step(nkb - 1, 1, False)

    def far_step(kb, cur_scr, nxt_scr):
        if nxt_scr is None:
            for h in range(N_HEADS):
                consume(kb, h, cur_scr)
            return
        kb_next = jnp.minimum(kb + 1, n_far - 1)

        @pl.when(kb_next >= 0)
        def _():
            set_mask_tile(kb_next, mb_scr)

        for h in range(-FAR_LOOKAHEAD, N_HEADS):
            if h + FAR_LOOKAHEAD < N_HEADS:
                produce(kb_next, h + FAR_LOOKAHEAD, nxt_scr, mb_scr)
            if h >= 0:
                consume(kb, h, cur_scr)

    def pair_body(t, carry):
        far_step(2 * t, sb_scr, sa_scr)
        far_step(2 * t + 1, sa_scr, sb_scr)
        return carry

    lax.fori_loop(0, n_far // 2, pair_body, 0)

    @pl.when(n_far % 2 == 1)
    def _():
        far_step(n_far - 1, sb_scr, None)

    def normalised(h):
        return acc_scr[h, :HEAD_DIM, :] / acc_scr[h, HEAD_DIM:HEAD_DIM + 1, :]

    for h in range(0, N_HEADS, 2):
        pair = jnp.concatenate([normalised(h), normalised(h + 1)], axis=0)
        o_ref[:, h * HEAD_DIM:(h + 2) * HEAD_DIM] = pair.T.astype(BF16)


def _dsa_cols(qt, qit, wit, k, vt, ki, bias, *, nb, seq, tq, n_sel):
    tk = tq
    nq = seq // tq
    assert tk >= REL_MAX_DIST and seq % tq == 0
    qmap = lambda b, j: (0, b * nq + j)
    once = lambda shape, imap: pl.BlockSpec(shape, imap, pipeline_mode=pl.Buffered(1))
    scratch = [pltpu.VMEM((nq, tk, tq), jnp.int16),
               pltpu.VMEM((nq, tk, tq), jnp.int16),
               pltpu.VMEM((nq, tk, tq), jnp.int16),
               pltpu.VMEM((3, tk, tq), F32),
               pltpu.VMEM((N_HEADS, 1, tq), F32),
               pltpu.VMEM((N_HEADS, HEAD_DIM + BF16_ROWS, tq), F32),
               pltpu.VMEM((tk, tq), F32),
               pltpu.VMEM((tk, tq), F32),
               pltpu.VMEM((N_HEADS, tk, tq), F32),
               pltpu.VMEM((N_HEADS, tk, tq), F32)]
    est = (seq * (2 * ATT_KV + V7X_LANES) * 2 + 3 * seq * tq * 2 + 4 * N_HEADS * tq * tk * 4
           + 2 * tq * (2 * ATT_Q + IDX_Q) * 2 + N_HEADS * HEAD_DIM * tq * 4
           + IDX_HEADS * tq * tk * 4 * 2 + 8 * tq * tk * 4)
    return pl.pallas_call(
        functools.partial(_dsa_cols_kernel, tq=tq, tk=tk, n_sel=n_sel),
        out_shape=jax.ShapeDtypeStruct((nb, seq, ATT_Q), BF16),
        grid=(nb, nq),
        in_specs=[pl.BlockSpec((ATT_Q, tq), qmap), pl.BlockSpec((IDX_Q, tq), qmap),
                  pl.BlockSpec((IDX_HEADS, tq), qmap),
                  once((None, seq, ATT_KV), lambda b, j: (b, 0, 0)),
                  once((None, nq, ATT_KV, tk), lambda b, j: (b, 0, 0, 0)),
                  once((None, seq, IDX_DIM), lambda b, j: (b, 0, 0)),
                  _resident(bias.shape)],
        out_specs=pl.BlockSpec((None, tq, ATT_Q), lambda b, j: (b, j, 0)),
        scratch_shapes=scratch,
        compiler_params=pltpu.CompilerParams(dimension_semantics=("arbitrary", "arbitrary"),
                                             vmem_limit_bytes=_vmem_limit(est)),
        name="dsa_cols",
    )(qt, qit, wit, k, vt, ki, bias)


def _dsa_rows_kernel(q_ref, qi_ref, wi_ref, k_ref, v_ref, ki_ref, bias_ref, o_ref,
                     keys_scr, qstk_scr, qistk_scr, wib_scr, *, tq, tk, n_sel, n_keys):
    f_sel = float(n_sel)
    nkb = -(-n_keys // tk)
    n_vis = jnp.full((tq, 1), n_keys, jnp.int32)

    for hh in range(N_HEADS):
        g, i = divmod(hh, GROUP)
        qstk_scr[g, i * tq:(i + 1) * tq, :] = q_ref[:, hh * HEAD_DIM:(hh + 1) * HEAD_DIM]
    for h in range(IDX_HEADS):
        qistk_scr[h * tq:(h + 1) * tq, :] = qi_ref[:, h * IDX_DIM:(h + 1) * IDX_DIM]
        wib_scr[h] = jnp.broadcast_to(wi_ref[:, h:h + 1], (tq, tk))

    nt_dims = (((1,), (1,)), ((), ()))

    def score_body(kb, carry):
        off = pl.multiple_of(kb * tk, tk)
        d = lax.dot_general(qistk_scr[...], ki_ref[pl.ds(off, tk), :], nt_dims, preferred_element_type=F32)
        s = jnp.zeros((tq, tk), F32)
        for h in range(IDX_HEADS):
            s = s + wib_scr[h] * jnp.maximum(d[h * tq:(h + 1) * tq], 0.0)
        kpos = off + lax.broadcasted_iota(jnp.int32, (tq, tk), 1)
        keys_scr[kb] = jnp.where(kpos < n_vis, _sortable_key(s), INT_MIN)
        return carry

    lax.fori_loop(0, nkb, score_body, 0)

    def count_ge(cand):
        candb = jnp.broadcast_to(cand, (tq, V7X_LANES))

        def body(kb, part):
            t = keys_scr[kb]
            for c in range(tk // V7X_LANES):
                part = part + jnp.where(t[:, c * V7X_LANES:(c + 1) * V7X_LANES] >= candb, 1.0, 0.0)
            return part

        part = lax.fori_loop(0, nkb, body, jnp.zeros((tq, V7X_LANES), F32))
        return jnp.sum(part, axis=1, keepdims=True)

    thr, cnt_thr = _kth_largest_key(count_ge, f_sel, bits=32, lowest=INT_MIN,
                                    cnt_lowest=jnp.zeros((tq, 1), F32))

    tie = (n_vis > n_sel) & (cnt_thr != f_sel)
    any_tie = jnp.max(jnp.where(tie, 1.0, 0.0)) > 0.0

    @pl.when(any_tie)
    def _():
        room = f_sel - count_ge(thr + 1)
        rr = lax.broadcasted_iota(jnp.int32, (tk, tk), 0)
        cc = lax.broadcasted_iota(jnp.int32, (tk, tk), 1)
        upper = jnp.where(rr <= cc, 1.0, 0.0).astype(BF16)

        def body(kb, seen):
            t = keys_scr[kb]
            eq = t == thr
            eqf = jnp.where(eq, 1.0, 0.0)
            rank = jnp.dot(eqf.astype(BF16), upper, preferred_element_type=F32) + seen
            keys_scr[kb] = jnp.where(eq & tie & (rank > room), thr - 1, t)
            return seen + jnp.sum(eqf, axis=1, keepdims=True)

        lax.fori_loop(0, nkb, body, jnp.zeros((tq, 1), F32))

    thr_sel = jnp.maximum(thr, INT_MIN + 1)

    lp = nkb * tk
    mask_add = jnp.concatenate([jnp.where(keys_scr[kb] >= thr_sel, 0.0, NEG_LOGIT) for kb in range(nkb)], axis=1)
    far = jnp.zeros((GROUP, tq, lp - 2 * tk), F32)
    for g in range(KV_HEADS):
        cols = slice(g * HEAD_DIM, (g + 1) * HEAD_DIM)
        heads = slice(g * GROUP, (g + 1) * GROUP)
        s = lax.dot_general(qstk_scr[g], k_ref[:, cols], nt_dims, preferred_element_type=F32)
        s = s.reshape(GROUP, tq, lp) + mask_add[None]
        s = s + jnp.concatenate([far, bias_ref[0, heads], bias_ref[1, heads]], axis=-1)
        p = jnp.exp2(s - jnp.max(s, axis=-1, keepdims=True))
        l = jnp.sum(p, axis=-1, keepdims=True)
        pv = jnp.dot(p.astype(BF16).reshape(GROUP * tq, lp), v_ref[:, cols], preferred_element_type=F32)
        o = pv.reshape(GROUP, tq, HEAD_DIM) / l
        for i in range(GROUP):
            hh = g * GROUP + i
            o_ref[:, hh * HEAD_DIM:(hh + 1) * HEAD_DIM] = o[i].astype(BF16)


def _dsa_rows(q, qi, wi, k, v, ki, bias, *, tq, tk, n_sel, q_start, n_keys):
    nb = q.shape[0]
    lp = k.shape[1]
    nkb = lp // tk
    qmap = lambda b: (b, 0, 0)
    assert tk >= REL_MAX_DIST and lp % tk == 0 and q_start == (nkb - 1) * tk and nkb == -(-n_keys // tk)
    scratch = [pltpu.VMEM((nkb, tq, tk), jnp.int32),
               pltpu.VMEM((KV_HEADS, GROUP * tq, HEAD_DIM), BF16),
               pltpu.VMEM((IDX_HEADS * tq, IDX_DIM), BF16),
               pltpu.VMEM((IDX_HEADS, tq, tk), F32)]
    est = (2 * lp * (2 * ATT_KV + V7X_LANES) * 2 + lp * tq * 4 + 2 * N_HEADS * tq * tk * 4
           + IDX_HEADS * tq * tk * 4 * 3 + 8 * N_HEADS * tq * lp * 4)
    return pl.pallas_call(
        functools.partial(_dsa_rows_kernel, tq=tq, tk=tk, n_sel=n_sel, n_keys=n_keys),
        out_shape=jax.ShapeDtypeStruct((nb, tq, ATT_Q), BF16),
        grid=(nb,),
        in_specs=[pl.BlockSpec((None, tq, ATT_Q), qmap), pl.BlockSpec((None, tq, IDX_Q), qmap),
                  pl.BlockSpec((None, tq, V7X_LANES), qmap),
                  pl.BlockSpec((None, lp, ATT_KV), qmap), pl.BlockSpec((None, lp, ATT_KV), qmap),
                  pl.BlockSpec((None, lp, IDX_DIM), qmap),
                  _resident(bias.shape)],
        out_specs=pl.BlockSpec((None, tq, ATT_Q), qmap),
        scratch_shapes=scratch,
        compiler_params=pltpu.CompilerParams(dimension_semantics=("arbitrary",),
                                             vmem_limit_bytes=_vmem_limit(est)),
        name="dsa_rows",
    )(q, qi, wi, k, v, ki, bias)


def _merge_ffn_kernel(a_ref, gate_ref, ga_ref, gb_ref, h_ref, wa_ref, wb_ref, wo_ref, g2_ref, b2_ref,
                      wg_ref, wu_ref, wd_ref, g3_ref, b3_ref, o_ref, act_scr, *, alpha):
    br_a = jnp.dot(a_ref[...], wa_ref[...], preferred_element_type=F32)
    br_b = jnp.dot(gate_ref[...], wb_ref[...], preferred_element_type=F32)
    merge = jax.nn.sigmoid(ga_ref[...]) * br_a + jax.nn.sigmoid(gb_ref[...]) * br_b
    mixed = jnp.dot(merge.astype(BF16), wo_ref[...], preferred_element_type=F32)
    h2 = _layer_norm(alpha * h_ref[...] + mixed, g2_ref[...], b2_ref[...])
    y = _swiglu(h2.astype(BF16), wg_ref, wu_ref, wd_ref, act_scr)
    o_ref[...] = _layer_norm(alpha * h2 + 0.5 * y, g3_ref[...], b3_ref[...])


def _merge_ffn(a, gate, ga, gb, h, wa, wb, wo, g2, b2, wg, wu, wd, g3, b3, *, tm, alpha):
    m = h.shape[0]
    assert m % tm == 0
    row = lambda i: (i, 0)
    tile = lambda: pl.BlockSpec((tm, D_MODEL), row)
    vec = lambda: _resident((1, D_MODEL))
    sq = lambda: _resident((D_MODEL, D_MODEL))
    weights = (3 * D_MODEL * D_MODEL + 3 * D_MODEL * D_FF) * 2
    tiles = tm * D_MODEL * 4 * 12 + tm * D_FF * (2 + 3 * 4)
    return pl.pallas_call(
        functools.partial(_merge_ffn_kernel, alpha=alpha),
        out_shape=jax.ShapeDtypeStruct((m, D_MODEL), F32),
        grid=(m // tm,),
        in_specs=[tile(), tile(), tile(), tile(), tile(), sq(), sq(), sq(), vec(), vec(),
                  _resident((D_MODEL, D_FF)), _resident((D_MODEL, D_FF)), _resident((D_FF, D_MODEL)),
                  vec(), vec()],
        out_specs=tile(),
        scratch_shapes=[pltpu.VMEM((tm, D_FF), BF16)],
        compiler_params=pltpu.CompilerParams(dimension_semantics=("arbitrary",),
                                             vmem_limit_bytes=_vmem_limit(weights + tiles)),
        name="merge_ffn",
    )(a, gate, ga, gb, h, wa, wb, wo, g2, b2, wg, wu, wd, g3, b3)


def _split_w_in(w_in):
    o_k = ATT_Q
    o_qi = ATT_Q + 2 * ATT_KV
    o_ki = o_qi + IDX_Q
    o_wi = o_ki + IDX_DIM
    o_zg = o_wi + IDX_HEADS
    o_ga = o_zg + 2 * D_MODEL
    o_gb = o_ga + D_MODEL
    wb = w_in.astype(BF16)
    pad = lambda w, n: jnp.pad(w, ((0, 0), (0, n - w.shape[1])))
    widx = jnp.concatenate([wb[:, o_qi:o_ki], pad(wb[:, o_ki:o_wi], V7X_LANES),
                            pad(wb[:, o_wi:o_zg], V7X_LANES)], axis=1)
    return wb[:, :o_qi], widx, wb[:, o_zg:o_ga], wb[:, o_ga:o_gb], wb[:, o_gb:]


def kernel(x_prompt, x_sample, cache_k, cache_v, cache_kidx, rel_table, ln1_g, ln1_b, ffn1_wg, ffn1_wu, ffn1_wd, w_in, gm_ln_g, gm_ln_b, gm_ws, gm_bs, w_br_a, w_br_b, w_out, ln2_g, ln2_b, ffn2_wg, ffn2_wu, ffn2_wd, ln3_g, ln3_b):
    depth = ln1_g.shape[0]
    assert depth == 1, "single-layer step"
    alpha = (2 * depth) ** 0.25
    nb, seq, _ = x_prompt.shape
    ns, n_new, _ = x_sample.shape
    past = cache_k.shape[2]
    total = past + n_new
    l = 0
    vec = lambda p: p[l].reshape(1, D_MODEL)
    bf = lambda p: p[l].astype(BF16)

    w_qkv, w_idx, w_zg, w_ga, w_gb = _split_w_in(w_in[l])
    f1 = (bf(ffn1_wg), bf(ffn1_wu), bf(ffn1_wd))
    f2 = (bf(ffn2_wg), bf(ffn2_wu), bf(ffn2_wd))
    br = (bf(w_br_a), bf(w_br_b), bf(w_out))

    def dense_in(x, tm, gate_rows, emit_vg, kv_block):
        h = _ffn_ln(x, *f1, vec(ln1_g), vec(ln1_b), tm=tm, alpha=alpha)
        reps = GM_CHUNK // gate_rows
        ws = jnp.tile(gm_ws[l][:, :gate_rows, :gate_rows], (1, reps, reps))
        bst = jnp.tile(gm_bs[l][:, :gate_rows].T, (reps, 1))
        parts = _mix_in(h, w_qkv, w_idx, w_zg, w_ga, w_gb, vec(gm_ln_g), vec(gm_ln_b), ws, bst,
                        tm=tm, gate_rows=gate_rows, emit_vg=emit_vg, kv_block=kv_block)
        return h, parts

    def dense_out(a, p, h, tm):
        return _merge_ffn(a, p["gate"], p["ga"], p["gb"], h, *br, vec(ln2_g), vec(ln2_b), *f2,
                          vec(ln3_g), vec(ln3_b), tm=tm, alpha=alpha)

    mp = nb * seq
    tq = 256
    hp, pp = dense_in(x_prompt.reshape(mp, D_MODEL), 512, GM_CHUNK, False, tq)
    bias_p = _bias_tiles(rel_table, tq, tq, 0)
    a = _dsa_cols(pp["q"], pp["qi"], pp["wi"], pp["k"].reshape(nb, seq, ATT_KV),
                  pp["v"].reshape(nb, seq // tq, ATT_KV, tq), pp["ki"].reshape(nb, seq, IDX_DIM), bias_p,
                  nb=nb, seq=seq, tq=tq, n_sel=min(TOPK_MAX, seq // 4))
    y_p = dense_out(a.reshape(mp, ATT_Q), pp, hp, 512).reshape(nb, seq, D_MODEL)

    ms = ns * n_new
    hs, ps = dense_in(x_sample.reshape(ms, D_MODEL), 128, n_new, True, None)
    tk_s = V7X_LANES
    lp = -(-total // tk_s) * tk_s
    cat = lambda c, new, w: jnp.pad(
        jnp.concatenate([c[l].reshape(ns, past, w).astype(BF16), new.reshape(ns, n_new, w)], axis=1),
        ((0, 0), (0, lp - total), (0, 0)))
    bias_s = _bias_tiles(rel_table, n_new, tk_s, 1)
    a = _dsa_rows(ps["q"].reshape(ns, n_new, ATT_Q), ps["qi"].reshape(ns, n_new, IDX_Q),
                  ps["wi"].reshape(ns, n_new, V7X_LANES),
                  cat(cache_k, ps["k"], ATT_KV), cat(cache_v, ps["v"], ATT_KV), cat(cache_kidx, ps["ki"], IDX_DIM),
                  bias_s, tq=n_new, tk=tk_s, n_sel=min(TOPK_MAX, total // 4), q_start=past, n_keys=total)
    y_s = dense_out(a.reshape(ms, ATT_Q), ps, hs, 128).reshape(ns, n_new, D_MODEL)

    kv5 = lambda x, b, t: x.reshape(1, b, t, KV_HEADS, HEAD_DIM)
    return (y_p, y_s,
            kv5(pp["k_f32"], nb, seq), kv5(pp["v_f32"], nb, seq), pp["ki_f32"].reshape(1, nb, seq, IDX_DIM),
            kv5(ps["k_f32"], ns, n_new), kv5(ps["v_f32"], ns, n_new), ps["ki_f32"].reshape(1, ns, n_new, IDX_DIM),
            ps["vg"].reshape(1, ns, n_new, D_MODEL))
```
